```python
import jax, jax.numpy as jnp
from jax import lax
import numpy as np

D_MODEL = 1024
BATCH = 16
SEQ = 2048
DEPTH = 4

N_EVEN = (DEPTH + 1) // 2
N_ODD = DEPTH // 2

NSA_HEADS = 8
NSA_KV_HEADS = 2
NSA_GROUP = NSA_HEADS // NSA_KV_HEADS
NSA_HEAD_DIM = 64
NSA_WIDTH = NSA_HEADS * NSA_HEAD_DIM
NSA_KV_WIDTH = NSA_KV_HEADS * NSA_HEAD_DIM
CMP_BLOCK = 32
CMP_STRIDE = 16
SLC_BLOCK = 64
SLC_TOPK = 8
WINDOW = 512
Q_BLOCK = 128
FORCE_BONUS = 1e4
NEG_INF = -1e30

SGU_GROUPS = 8
SGU_GROUP_DIM = 64
SGU_WIDTH = SGU_GROUPS * SGU_GROUP_DIM
SGU_CHUNK = 128

CONV_WIDTH = D_MODEL
CONV_KERNEL = 31

RMS_EPS = 1e-6
LN_EPS = 1e-5

EVEN_SPLIT_SIZES = (NSA_WIDTH, NSA_KV_WIDTH, NSA_KV_WIDTH, NSA_KV_WIDTH, NSA_KV_WIDTH,
                    NSA_KV_WIDTH, NSA_KV_WIDTH, 3 * NSA_HEADS, NSA_WIDTH,
                    SGU_WIDTH, SGU_WIDTH, SGU_WIDTH)
EVEN_IN_WIDTH = sum(EVEN_SPLIT_SIZES)
EVEN_MIX_WIDTH = NSA_WIDTH + SGU_WIDTH

kernel_name = "nsa_sgu_conformer_hybrid"


def _offsets(sizes):
    out, acc = [], 0
    for s in sizes[:-1]:
        acc += s
        out.append(acc)
    return out


def rms_norm(x, g):
    x32 = x.astype(jnp.float32)
    y = x32 * lax.rsqrt(jnp.mean(x32 * x32, axis=-1, keepdims=True) + RMS_EPS)
    return (y * g.astype(jnp.float32)).astype(x.dtype)


def layer_norm(x, g, b):
    x32 = x.astype(jnp.float32)
    mu = jnp.mean(x32, axis=-1, keepdims=True)
    xc = x32 - mu
    y = xc * lax.rsqrt(jnp.mean(xc * xc, axis=-1, keepdims=True) + LN_EPS)
    return (y * g.astype(jnp.float32) + b.astype(jnp.float32)).astype(x.dtype)


def masked_softmax(s, mask):
    p = jax.nn.softmax(jnp.where(mask, s, NEG_INF), axis=-1)
    return jnp.where(mask, p, 0.0)


def compress_blocks(k, pe, w1, w2):
    B, S, G, dh = k.shape
    n_cmp = (S - CMP_BLOCK) // CMP_STRIDE + 1
    idx = np.arange(n_cmp)[:, None] * CMP_STRIDE + np.arange(CMP_BLOCK)[None, :]
    blk = k[:, idx] + pe[None, None, :, None, :]
    blk = jnp.swapaxes(blk, 2, 3).reshape(B, n_cmp, G, CMP_BLOCK * dh)
    return jax.nn.silu(blk @ w1) @ w2


def nsa_attention(q, kc, vc, ks, vs, kw, vw, gate_logits,
                  cmp_k_pe, cmp_k_w1, cmp_k_w2, cmp_v_pe, cmp_v_w1, cmp_v_w2):
    B, S, _ = q.shape
    G, R, dh = NSA_KV_HEADS, NSA_GROUP, NSA_HEAD_DIM
    q = q.reshape(B, S, G, R, dh) * (dh ** -0.5)
    kc, vc, ks, vs, kw, vw = [a.reshape(B, S, G, dh) for a in (kc, vc, ks, vs, kw, vw)]
    t = np.arange(S)

    k_cmp = compress_blocks(kc, cmp_k_pe, cmp_k_w1, cmp_k_w2)
    v_cmp = compress_blocks(vc, cmp_v_pe, cmp_v_w1, cmp_v_w2)
    n_cmp = k_cmp.shape[1]
    cmp_end = np.arange(n_cmp) * CMP_STRIDE + CMP_BLOCK - 1
    cmp_mask = cmp_end[None, :] <= t[:, None]
    s_cmp = jnp.einsum('bsgrd,bngd->bgrsn', q, k_cmp).astype(jnp.float32)
    p_cmp = masked_softmax(s_cmp, cmp_mask)
    o_cmp = jnp.einsum('bgrsn,bngd->bsgrd', p_cmp.astype(v_cmp.dtype), v_cmp)

    n_slc = S // SLC_BLOCK
    k_top = min(SLC_TOPK, n_slc)
    tok = np.arange(n_cmp)[:, None] * CMP_STRIDE + np.arange(CMP_BLOCK)[None, :]
    overlap = ((tok[:, :, None] // SLC_BLOCK) == np.arange(n_slc)[None, None, :]).sum(1)
    overlap = overlap.astype(np.float32) / np.float32(CMP_BLOCK)
    imp = jnp.einsum('bgrsn,nj->bgsj', p_cmp, overlap)
    cur = t // SLC_BLOCK
    j = np.arange(n_slc)
    forced = (j[None, :] == 0) | (j[None, :] == cur[:, None]) | (j[None, :] == cur[:, None] - 1)
    causal_blk = j[None, :] <= cur[:, None]
    imp = jnp.where(causal_blk, imp + forced.astype(np.float32) * np.float32(FORCE_BONUS), NEG_INF)
    _, sel = lax.top_k(imp, k_top)

    ks_blk = ks.reshape(B, n_slc, SLC_BLOCK, G, dh).transpose(0, 3, 1, 2, 4)
    vs_blk = vs.reshape(B, n_slc, SLC_BLOCK, G, dh).transpose(0, 3, 1, 2, 4)
    kw_pad = jnp.pad(kw, ((0, 0), (WINDOW, 0), (0, 0), (0, 0)))
    vw_pad = jnp.pad(vw, ((0, 0), (WINDOW, 0), (0, 0), (0, 0)))
    bi = jnp.arange(B)[:, None, None, None]
    gi = jnp.arange(G)[None, :, None, None]

    def block_fn(qb):
        s0 = qb * Q_BLOCK
        q_blk = lax.dynamic_slice_in_dim(q, s0, Q_BLOCK, axis=1)
        tq = s0 + jnp.arange(Q_BLOCK)
        sel_blk = lax.dynamic_slice_in_dim(sel, s0, Q_BLOCK, axis=2)
        k_sel = ks_blk[bi, gi, sel_blk].reshape(B, G, Q_BLOCK, k_top * SLC_BLOCK, dh)
        v_sel = vs_blk[bi, gi, sel_blk].reshape(B, G, Q_BLOCK, k_top * SLC_BLOCK, dh)
        pos = (sel_blk[..., None] * SLC_BLOCK + jnp.arange(SLC_BLOCK)).reshape(B, G, Q_BLOCK, -1)
        m_sel = (pos <= tq[None, None, :, None])[:, :, None]
        s_sel = jnp.einsum('bqgrd,bgqkd->bgrqk', q_blk, k_sel).astype(jnp.float32)
        p_sel = masked_softmax(s_sel, m_sel)
        o_sel = jnp.einsum('bgrqk,bgqkd->bqgrd', p_sel.astype(v_sel.dtype), v_sel)
        k_win = lax.dynamic_slice_in_dim(kw_pad, s0, Q_BLOCK + WINDOW, axis=1)
        v_win = lax.dynamic_slice_in_dim(vw_pad, s0, Q_BLOCK + WINDOW, axis=1)
        kp = s0 - WINDOW + jnp.arange(Q_BLOCK + WINDOW)
        m_win = ((kp[None, :] <= tq[:, None]) & (kp[None, :] > tq[:, None] - WINDOW)
                 & (kp[None, :] >= 0))
        s_win = jnp.einsum('bqgrd,bkgd->bgrqk', q_blk, k_win).astype(jnp.float32)
        p_win = masked_softmax(s_win, m_win)
        o_win = jnp.einsum('bgrqk,bkgd->bqgrd', p_win.astype(v_win.dtype), v_win)
        return o_sel, o_win

    o_sel, o_win = lax.map(block_fn, jnp.arange(S // Q_BLOCK))
    o_sel = jnp.moveaxis(o_sel, 0, 1).reshape(B, S, G, R, dh)
    o_win = jnp.moveaxis(o_win, 0, 1).reshape(B, S, G, R, dh)

    gates = jax.nn.sigmoid(gate_logits.astype(jnp.float32)).reshape(B, S, 3, G, R, 1).astype(q.dtype)
    o = gates[:, :, 0] * o_cmp + gates[:, :, 1] * o_sel + gates[:, :, 2] * o_win
    return o.reshape(B, S, NSA_WIDTH)


def spatial_gating(u, v, ln_g, ln_b, w_s, b_s):
    B, S, _ = u.shape
    v = layer_norm(v, ln_g, ln_b)
    v = v.reshape(B, S // SGU_CHUNK, SGU_CHUNK, SGU_GROUPS, SGU_GROUP_DIM)
    mixed = jnp.einsum('gij,bnjgd->bnigd', jnp.tril(w_s), v) + b_s.T[None, None, :, :, None]
    return u * mixed.reshape(B, S, SGU_WIDTH)


def even_layer(h, w_in, cmp_k_pe, cmp_k_w1, cmp_k_w2, cmp_v_pe, cmp_v_w1, cmp_v_w2,
               sgu_ln_g, sgu_ln_b, sgu_w, sgu_b, w_out):
    parts = jnp.split(h @ w_in, _offsets(EVEN_SPLIT_SIZES), axis=-1)
    q, kc, vc, ks, vs, kw, vw, g_logit, z_a, u, v, z_b = parts
    a = nsa_attention(q, kc, vc, ks, vs, kw, vw, g_logit,
                      cmp_k_pe, cmp_k_w1, cmp_k_w2, cmp_v_pe, cmp_v_w1, cmp_v_w2) * jax.nn.silu(z_a)
    b = spatial_gating(jax.nn.gelu(u), jax.nn.gelu(v), sgu_ln_g, sgu_ln_b, sgu_w, sgu_b) * jax.nn.silu(z_b)
    return jnp.concatenate([a, b], axis=-1) @ w_out


def conv_layer(h, w_in, dw_w, dw_b, ln_g, ln_b, w_out):
    a, gl, z = jnp.split(h @ w_in, 3, axis=-1)
    y = a * jax.nn.sigmoid(gl)
    y = lax.conv_general_dilated(y, dw_w[:, None, :], window_strides=(1,),
                                 padding=((CONV_KERNEL - 1, 0),),
                                 dimension_numbers=('NWC', 'WIO', 'NWC'),
                                 feature_group_count=CONV_WIDTH) + dw_b
    y = jax.nn.silu(layer_norm(y, ln_g, ln_b))
    return (y * jax.nn.silu(z)) @ w_out


def setup_inputs(seed: int = 0) -> dict:
    key = jax.random.key(seed)
    ks = jax.random.split(key, 24)

    def nrm(k, shape, scale):
        return jax.random.normal(k, shape, jnp.float32) * scale

    dh = NSA_HEAD_DIM
    return {
        "x": nrm(ks[0], (BATCH, SEQ, D_MODEL), 1.0),
        "norm_pre": 1.0 + nrm(ks[1], (DEPTH, D_MODEL), 0.01),
        "norm_post": 1.0 + nrm(ks[2], (DEPTH, D_MODEL), 0.01),
        "even_w_in": nrm(ks[3], (N_EVEN, D_MODEL, EVEN_IN_WIDTH), D_MODEL ** -0.5),
        "even_cmp_k_pe": nrm(ks[4], (N_EVEN, CMP_BLOCK, dh), 0.02),
        "even_cmp_k_w1": nrm(ks[5], (N_EVEN, CMP_BLOCK * dh, dh), (CMP_BLOCK * dh) ** -0.5),
        "even_cmp_k_w2": nrm(ks[6], (N_EVEN, dh, dh), dh ** -0.5),
        "even_cmp_v_pe": nrm(ks[7], (N_EVEN, CMP_BLOCK, dh), 0.02),
        "even_cmp_v_w1": nrm(ks[8], (N_EVEN, CMP_BLOCK * dh, dh), (CMP_BLOCK * dh) ** -0.5),
        "even_cmp_v_w2": nrm(ks[9], (N_EVEN, dh, dh), dh ** -0.5),
        "even_sgu_ln_g": 1.0 + nrm(ks[10], (N_EVEN, SGU_WIDTH), 0.01),
        "even_sgu_ln_b": nrm(ks[11], (N_EVEN, SGU_WIDTH), 0.01),
        "even_sgu_w": nrm(ks[12], (N_EVEN, SGU_GROUPS, SGU_CHUNK, SGU_CHUNK), SGU_CHUNK ** -0.5),
        "even_sgu_b": 1.0 + nrm(ks[13], (N_EVEN, SGU_GROUPS, SGU_CHUNK), 0.01),
        "even_w_out": nrm(ks[14], (N_EVEN, EVEN_MIX_WIDTH, D_MODEL), EVEN_MIX_WIDTH ** -0.5),
        "odd_w_in": nrm(ks[15], (N_ODD, D_MODEL, 3 * CONV_WIDTH), D_MODEL ** -0.5),
        "odd_dw_w": nrm(ks[16], (N_ODD, CONV_KERNEL, CONV_WIDTH), CONV_KERNEL ** -0.5),
        "odd_dw_b": nrm(ks[17], (N_ODD, CONV_WIDTH), 0.01),
        "odd_ln_g": 1.0 + nrm(ks[18], (N_ODD, CONV_WIDTH), 0.01),
        "odd_ln_b": nrm(ks[19], (N_ODD, CONV_WIDTH), 0.01),
        "odd_w_out": nrm(ks[20], (N_ODD, CONV_WIDTH, D_MODEL), CONV_WIDTH ** -0.5),
    }


def reference(x, norm_pre, norm_post, even_w_in, even_cmp_k_pe, even_cmp_k_w1, even_cmp_k_w2,
              even_cmp_v_pe, even_cmp_v_w1, even_cmp_v_w2, even_sgu_ln_g, even_sgu_ln_b,
              even_sgu_w, even_sgu_b, even_w_out, odd_w_in, odd_dw_w, odd_dw_b, odd_ln_g,
              odd_ln_b, odd_w_out):
    for i in range(DEPTH):
        h = rms_norm(x, norm_pre[i])
        li = i // 2
        if i % 2 == 0:
            y = even_layer(h, even_w_in[li], even_cmp_k_pe[li], even_cmp_k_w1[li], even_cmp_k_w2[li],
                           even_cmp_v_pe[li], even_cmp_v_w1[li], even_cmp_v_w2[li],
                           even_sgu_ln_g[li], even_sgu_ln_b[li], even_sgu_w[li], even_sgu_b[li],
                           even_w_out[li])
        else:
            y = conv_layer(h, odd_w_in[li], odd_dw_w[li], odd_dw_b[li], odd_ln_g[li],
                           odd_ln_b[li], odd_w_out[li])
        x = x + rms_norm(y, norm_post[i])
    return x
```

```python
import functools

import jax
import jax.numpy as jnp
import numpy as np
from jax import lax
from jax.experimental import pallas as pl
from jax.experimental.pallas import tpu as pltpu

N_HEADS = 8
N_KV = 2
GROUP = N_HEADS // N_KV
HEAD_DIM = 64
NSA_W = N_HEADS * HEAD_DIM
KV_W = N_KV * HEAD_DIM
CMP_BLOCK = 32
CMP_STRIDE = 16
SLC_BLOCK = 64
SLC_TOPK = 8
WINDOW = 512
FORCE_BONUS = 1e4
NEG_INF = -1e30
SGU_GROUPS = 8
SGU_GROUP_DIM = 64
SGU_W = SGU_GROUPS * SGU_GROUP_DIM
SGU_CHUNK = 128
CONV_K = 31
RMS_EPS = 1e-6
LN_EPS = 1e-5

LANES = 128
SUBLANES = 8
VMEM_LIMIT_BYTES = 56 * 1024 * 1024

IN_ROWS = 512
ATT_Q = 128
ATT_KC = 256
OUT_ROWS = 256
ODD_ROWS = 512
CONV_HALO = 32
CONV_RB = 16
CONV_SPAN = CONV_K + SUBLANES - 1

F32 = jnp.float32
BF16 = jnp.bfloat16


def _cparams(sem):
    return pltpu.CompilerParams(dimension_semantics=sem, vmem_limit_bytes=VMEM_LIMIT_BYTES)


def _rms(x, g):
    return x * lax.rsqrt(jnp.mean(x * x, axis=-1, keepdims=True) + RMS_EPS) * g


def _layer_norm(x, g, b):
    mu = jnp.mean(x, axis=-1, keepdims=True)
    xc = x - mu
    return xc * lax.rsqrt(jnp.mean(xc * xc, axis=-1, keepdims=True) + LN_EPS) * g + b


def _dot(a, b):
    return jnp.dot(a, b, preferred_element_type=F32)


def _dot_nt(a, b):
    return lax.dot_general(a, b, (((1,), (1,)), ((), ())), preferred_element_type=F32)


EV_Q, EV_KC, EV_KS, EV_ZA, EV_U, EV_V, EV_ZB, EV_G, EV_END = 0, 512, 768, 1280, 1792, 2304, 2816, 3328, 3456


def _even_in_kernel(x_ref, g_ref, w_ref, lng_ref, lnb_ref,
                    q_ref, kcvc_ref, kvsw_ref, sa_ref, usb_ref, vn_ref, gate_ref):
    h = _rms(x_ref[...], g_ref[...]).astype(BF16)
    q_ref[...] = (_dot(h, w_ref[:, EV_Q:EV_KC]) * (HEAD_DIM ** -0.5)).astype(BF16)
    kcvc_ref[...] = _dot(h, w_ref[:, EV_KC:EV_KS]).astype(BF16)
    kvsw_ref[...] = _dot(h, w_ref[:, EV_KS:EV_ZA]).astype(BF16)
    sa_ref[...] = jax.nn.silu(_dot(h, w_ref[:, EV_ZA:EV_U])).astype(BF16)
    u = _dot(h, w_ref[:, EV_U:EV_V])
    zb = _dot(h, w_ref[:, EV_ZB:EV_G])
    usb_ref[...] = (jax.nn.gelu(u) * jax.nn.silu(zb)).astype(BF16)
    v = jax.nn.gelu(_dot(h, w_ref[:, EV_V:EV_ZB]))
    vn_ref[...] = _layer_norm(v, lng_ref[...], lnb_ref[...]).astype(BF16)
    gate_ref[...] = jax.nn.sigmoid(_dot(h, w_ref[:, EV_G:EV_END]))


def _even_in(x2, g_pre, w_in, ln_g, ln_b):
    t, d = x2.shape
    w = jnp.concatenate(
        [w_in[:, :1280], w_in[:, 1304:], w_in[:, 1280:1304],
         jnp.zeros((d, LANES - 3 * N_HEADS), w_in.dtype)], axis=1).astype(BF16)
    row = lambda i: (i, 0)
    fixed = lambda i: (0, 0)
    outs = [(NSA_W, BF16), (2 * KV_W, BF16), (4 * KV_W, BF16), (NSA_W, BF16),
            (SGU_W, BF16), (SGU_W, BF16), (LANES, F32)]
    return pl.pallas_call(
        _even_in_kernel,
        grid=(t // IN_ROWS,),
        in_specs=[pl.BlockSpec((IN_ROWS, d), row), pl.BlockSpec((1, d), fixed),
                  pl.BlockSpec((d, EV_END), fixed), pl.BlockSpec((1, SGU_W), fixed),
                  pl.BlockSpec((1, SGU_W), fixed)],
        out_specs=[pl.BlockSpec((IN_ROWS, n), row) for n, _ in outs],
        out_shape=[jax.ShapeDtypeStruct((t, n), dt) for n, dt in outs],
        compiler_params=_cparams(("arbitrary",)),
        name="even_in",
    )(x2, g_pre.reshape(1, d), w, ln_g.reshape(1, SGU_W), ln_b.reshape(1, SGU_W))


def _compress_kernel(kr_ref, pea_ref, peb_ref, w1a_ref, w1b_ref, w2_ref, kcmp_ref, vcmp_ref):
    kr = kr_ref[0].astype(F32)
    n_rows = kr.shape[0]
    first = _dot((kr + pea_ref[...]).astype(BF16), w1a_ref[...])
    second = _dot((kr + peb_ref[...]).astype(BF16), w1b_ref[...])
    pre = first + pltpu.roll(second, n_rows - 1, 0)
    out = _dot(jax.nn.silu(pre).astype(BF16), w2_ref[...])
    rows = lax.broadcasted_iota(jnp.int32, out.shape, 0)
    out = jnp.where(rows < n_rows - 1, out, 0.0)
    kcmp_ref[0] = out[:, :KV_W].astype(BF16)
    vcmp_ref[0] = out[:, KV_W:].astype(BF16)


def _compress_weights(k_pe, k_w1, k_w2, v_pe, v_w1, v_w2):
    dh, half = HEAD_DIM, CMP_STRIDE
    eye_g = jnp.eye(N_KV, dtype=F32)

    def first_layer(w1, lo):
        w = w1.reshape(CMP_BLOCK, dh, dh)[lo:lo + half]
        return jnp.einsum('lde,gh->lgdhe', w, eye_g).reshape(half, KV_W, KV_W)

    def both(lo):
        wk, wv = first_layer(k_w1, lo), first_layer(v_w1, lo)
        z = jnp.zeros_like(wk)
        top = jnp.concatenate([wk, z], axis=2)
        bot = jnp.concatenate([z, wv], axis=2)
        return jnp.concatenate([top, bot], axis=1).reshape(half * 2 * KV_W, 2 * KV_W).astype(BF16)

    def pe_row(lo):
        pk = jnp.tile(k_pe[lo:lo + half], (1, N_KV))
        pv = jnp.tile(v_pe[lo:lo + half], (1, N_KV))
        return jnp.concatenate([pk, pv], axis=1).reshape(1, half * 2 * KV_W)

    w2 = jnp.zeros((2 * KV_W, 2 * KV_W), F32)
    for i, w in enumerate((k_w2, k_w2, v_w2, v_w2)):
        w2 = w2.at[i * dh:(i + 1) * dh, i * dh:(i + 1) * dh].set(w)
    return pe_row(0), pe_row(half), both(0), both(half), w2.astype(BF16)


def _compress(kcvc, batch, seq, cw):
    pea, peb, w1a, w1b, w2 = cw
    n_rows = seq // CMP_STRIDE
    width = CMP_STRIDE * 2 * KV_W
    kr = kcvc.reshape(batch, n_rows, width)
    fixed = lambda b: (0, 0)
    per_b = lambda b: (b, 0, 0)
    return pl.pallas_call(
        _compress_kernel,
        grid=(batch,),
        in_specs=[pl.BlockSpec((1, n_rows, width), per_b),
                  pl.BlockSpec((1, width), fixed), pl.BlockSpec((1, width), fixed),
                  pl.BlockSpec((width, 2 * KV_W), fixed), pl.BlockSpec((width, 2 * KV_W), fixed),
                  pl.BlockSpec((2 * KV_W, 2 * KV_W), fixed)],
        out_specs=[pl.BlockSpec((1, n_rows, KV_W), per_b)] * 2,
        out_shape=[jax.ShapeDtypeStruct((batch, n_rows, KV_W), BF16)] * 2,
        compiler_params=_cparams(("arbitrary",)),
        name="compress",
    )(kr, pea, peb, w1a, w1b, w2)


def _overlap_matrix(n_rows):
    n_cmp = n_rows - 1
    tok = np.arange(n_cmp)[:, None] * CMP_STRIDE + np.arange(CMP_BLOCK)[None, :]
    ov = ((tok[:, :, None] // SLC_BLOCK) == np.arange(LANES)[None, None, :]).sum(1)
    out = np.zeros((n_rows, LANES), np.float32)
    out[:n_cmp] = ov.astype(np.float32) / np.float32(CMP_BLOCK)
    return out


def _attn_kernel(q_ref, kv_ref, kcmp_ref, vcmp_ref, gate_ref, ov_ref, o_ref, *, seq):
    tq = ATT_Q
    t0 = pl.program_id(1) * tq
    lane = lax.broadcasted_iota(jnp.int32, (tq, LANES), 1)
    tok1 = t0 + lax.broadcasted_iota(jnp.int32, (tq, 1), 0)
    tok4 = jnp.concatenate([tok1] * GROUP, axis=0)
    gates = gate_ref[...]
    n_chunks = (t0 + tq + ATT_KC - 1) // ATT_KC
    w_start = pl.multiple_of(jnp.maximum(t0 - WINDOW, 0), tq)
    w_len = WINDOW + tq

    out_blocks = [None] * (N_HEADS // 2)
    for g in range(N_KV):
        in_half = (lane // HEAD_DIM) == g
        qs = []
        for r in range(GROUP):
            h = g * GROUP + r
            blk = q_ref[0, :, (h // 2) * LANES:(h // 2 + 1) * LANES].astype(F32)
            if h % 2 != g:
                blk = pltpu.roll(blk, HEAD_DIM, 1)
            qs.append(jnp.where(in_half, blk, 0.0).astype(BF16))
        qg = jnp.concatenate(qs, axis=0)

        s = _dot_nt(qg, kcmp_ref[0])
        n_idx = lax.broadcasted_iota(jnp.int32, s.shape, 1)
        valid = n_idx * CMP_STRIDE + (CMP_BLOCK - 1) <= tok4
        s = jnp.where(valid, s, NEG_INF)
        p = jnp.exp(s - jnp.max(s, axis=-1, keepdims=True))
        p = jnp.where(valid, p / jnp.sum(p, axis=-1, keepdims=True), 0.0)
        o_cmp = _dot(p.astype(BF16), vcmp_ref[0])

        psum = p[0:tq] + p[tq:2 * tq] + p[2 * tq:3 * tq] + p[3 * tq:4 * tq]
        imp = jnp.dot(psum, ov_ref[...], preferred_element_type=F32,
                      precision=lax.Precision.HIGHEST)
        cur = tok1 // SLC_BLOCK
        forced = (lane == 0) | (lane == cur) | (lane == cur - 1)
        imp = jnp.where(lane <= cur, imp + jnp.where(forced, FORCE_BONUS, 0.0), NEG_INF)
        rank = jnp.zeros((tq, LANES), F32)
        for j in range(seq // SLC_BLOCK):
            col = imp[:, j:j + 1]
            ahead = (col > imp) | ((col == imp) & (lane > j))
            rank = rank + jnp.where(ahead, 1.0, 0.0)
        selmask = jnp.where(rank < SLC_TOPK, 1.0, 0.0).astype(BF16)

        def sel_body(c, carry):
            m, l, acc = carry
            k0 = pl.multiple_of(c * ATT_KC, ATT_KC)
            kblk = kv_ref[0, pl.ds(k0, ATT_KC), 0:KV_W]
            vblk = kv_ref[0, pl.ds(k0, ATT_KC), KV_W:2 * KV_W]
            key = k0 + lax.broadcasted_iota(jnp.int32, (tq, ATT_KC), 1)
            blk_of_key = lax.broadcasted_iota(jnp.int32, (LANES, ATT_KC), 1) // SLC_BLOCK + k0 // SLC_BLOCK
            expand = (lax.broadcasted_iota(jnp.int32, (LANES, ATT_KC), 0) == blk_of_key)
            picked = _dot(selmask, jnp.where(expand, 1.0, 0.0).astype(BF16)) > 0.5
            mask1 = picked & (key <= tok1)
            mask = jnp.concatenate([mask1] * GROUP, axis=0)
            sc = jnp.where(mask, _dot_nt(qg, kblk), NEG_INF)
            m_new = jnp.maximum(m, jnp.max(sc, axis=-1, keepdims=True))
            alpha = jnp.exp(m - m_new)
            pr = jnp.where(mask, jnp.exp(sc - m_new), 0.0)
            l = alpha * l + jnp.sum(pr, axis=-1, keepdims=True)
            acc = alpha * acc + _dot(pr.astype(BF16), vblk)
            return m_new, l, acc

        init = (jnp.full((GROUP * tq, 1), NEG_INF, F32), jnp.zeros((GROUP * tq, 1), F32),
                jnp.zeros((GROUP * tq, LANES), F32))
        _, l_sel, acc_sel = lax.fori_loop(0, n_chunks, sel_body, init)
        o_sel = acc_sel / l_sel

        kw = kv_ref[0, pl.ds(w_start, w_len), 2 * KV_W:3 * KV_W]
        vw = kv_ref[0, pl.ds(w_start, w_len), 3 * KV_W:4 * KV_W]
        kpos = w_start + lax.broadcasted_iota(jnp.int32, (GROUP * tq, w_len), 1)
        wmask = (kpos <= tok4) & (kpos > tok4 - WINDOW)
        sw = jnp.where(wmask, _dot_nt(qg, kw), NEG_INF)
        pw = jnp.where(wmask, jnp.exp(sw - jnp.max(sw, axis=-1, keepdims=True)), 0.0)
        o_win = _dot(pw.astype(BF16), vw) / jnp.sum(pw, axis=-1, keepdims=True)

        for r in range(GROUP):
            h = g * GROUP + r
            rows = slice(r * tq, (r + 1) * tq)
            mixed = (gates[:, h:h + 1] * o_cmp[rows]
                     + gates[:, N_HEADS + h:N_HEADS + h + 1] * o_sel[rows]
                     + gates[:, 2 * N_HEADS + h:2 * N_HEADS + h + 1] * o_win[rows])
            if h % 2 != g:
                mixed = pltpu.roll(mixed, HEAD_DIM, 1)
            keep = (lane // HEAD_DIM) == (h % 2)
            prev = out_blocks[h // 2]
            out_blocks[h // 2] = jnp.where(keep, mixed, 0.0 if prev is None else prev)

    for i, blk in enumerate(out_blocks):
        o_ref[0, :, i * LANES:(i + 1) * LANES] = blk.astype(BF16)


def _attention(q, kvsw, kcmp, vcmp, gates, batch, seq):
    n_rows = seq // CMP_STRIDE
    ov = jnp.asarray(_overlap_matrix(n_rows))
    tile = lambda b, t: (b, t, 0)
    per_b = lambda b, t: (b, 0, 0)
    return pl.pallas_call(
        functools.partial(_attn_kernel, seq=seq),
        grid=(batch, seq // ATT_Q),
        in_specs=[pl.BlockSpec((1, ATT_Q, NSA_W), tile),
                  pl.BlockSpec((1, seq, 4 * KV_W), per_b),
                  pl.BlockSpec((1, n_rows, KV_W), per_b),
                  pl.BlockSpec((1, n_rows, KV_W), per_b),
                  pl.BlockSpec((ATT_Q, LANES), lambda b, t: (b * (seq // ATT_Q) + t, 0)),
                  pl.BlockSpec((n_rows, LANES), lambda b, t: (0, 0))],
        out_specs=pl.BlockSpec((1, ATT_Q, NSA_W), tile),
        out_shape=jax.ShapeDtypeStruct((batch, seq, NSA_W), BF16),
        compiler_params=_cparams(("arbitrary", "arbitrary")),
        name="nsa_attention",
    )(q.reshape(batch, seq, NSA_W), kvsw.reshape(batch, seq, 4 * KV_W), kcmp, vcmp, gates, ov)


def _even_out_kernel(a_ref, sa_ref, usb_ref, vn_ref, ws_ref, bias_ref, w_ref, g_ref, x_ref, o_ref):
    lane = lax.broadcasted_iota(jnp.int32, (SGU_CHUNK, LANES), 1)
    low = lane < SGU_GROUP_DIM
    ri = lax.broadcasted_iota(jnp.int32, (SGU_CHUNK, SGU_CHUNK), 0)
    ci = lax.broadcasted_iota(jnp.int32, (SGU_CHUNK, SGU_CHUNK), 1)
    tril = ci <= ri
    wmix = [jnp.where(tril, ws_ref[g], 0.0).astype(BF16) for g in range(SGU_GROUPS)]
    zero = jnp.zeros((), BF16)
    left = (a_ref[...].astype(F32) * sa_ref[...].astype(F32)).astype(BF16)
    rights = []
    for c in range(OUT_ROWS // SGU_CHUNK):
        rows = slice(c * SGU_CHUNK, (c + 1) * SGU_CHUNK)
        blocks = []
        for p in range(SGU_W // LANES):
            cols = slice(p * LANES, (p + 1) * LANES)
            vb = vn_ref[rows, cols]
            mixed = (_dot(wmix[2 * p], jnp.where(low, vb, zero))
                     + _dot(wmix[2 * p + 1], jnp.where(low, zero, vb))
                     + bias_ref[:, cols])
            blocks.append((usb_ref[rows, cols].astype(F32) * mixed).astype(BF16))
        rights.append(jnp.concatenate(blocks, axis=1))
    right = jnp.concatenate(rights, axis=0)
    y = _dot(left, w_ref[0:NSA_W, :]) + _dot(right, w_ref[NSA_W:, :])
    o_ref[...] = x_ref[...] + _rms(y, g_ref[...])


def _even_out(a, sa, usb, vn, sgu_w, sgu_b, w_out, g_post, x2):
    t, d = x2.shape
    bias = jnp.repeat(sgu_b.T, SGU_GROUP_DIM, axis=1)
    row = lambda i: (i, 0)
    fixed = lambda i: (0, 0)
    return pl.pallas_call(
        _even_out_kernel,
        grid=(t // OUT_ROWS,),
        in_specs=[pl.BlockSpec((OUT_ROWS, NSA_W), row), pl.BlockSpec((OUT_ROWS, NSA_W), row),
                  pl.BlockSpec((OUT_ROWS, SGU_W), row), pl.BlockSpec((OUT_ROWS, SGU_W), row),
                  pl.BlockSpec((SGU_GROUPS, SGU_CHUNK, SGU_CHUNK), lambda i: (0, 0, 0)),
                  pl.BlockSpec((SGU_CHUNK, SGU_W), fixed),
                  pl.BlockSpec((NSA_W + SGU_W, d), fixed), pl.BlockSpec((1, d), fixed),
                  pl.BlockSpec((OUT_ROWS, d), row)],
        out_specs=pl.BlockSpec((OUT_ROWS, d), row),
        out_shape=jax.ShapeDtypeStruct((t, d), F32),
        compiler_params=_cparams(("arbitrary",)),
        name="even_out",
    )(a, sa, usb, vn, sgu_w, bias, w_out.astype(BF16), g_post.reshape(1, d), x2)


def _odd_kernel(x_ref, gpre_ref, win_ref, dww_ref, dwb_ref, lng_ref, lnb_ref, wout_ref, gpost_ref,
                o_ref, ypad_ref, conv_ref):
    d = x_ref.shape[-1]
    x = x_ref[0]
    h = _rms(x, gpre_ref[...]).astype(BF16)
    a = _dot(h, win_ref[:, 0:d])
    gl = _dot(h, win_ref[:, d:2 * d])

    halo_blocks = CONV_HALO // SUBLANES
    tile_blocks = ODD_ROWS // SUBLANES

    @pl.when(pl.program_id(1) == 0)
    def _():
        ypad_ref[0:halo_blocks] = jnp.zeros((halo_blocks, SUBLANES, d), F32)

    ypad_ref[halo_blocks:] = (a * jax.nn.sigmoid(gl)).reshape(tile_blocks, SUBLANES, d)

    shift = CONV_HALO - (CONV_K - 1)
    blocks_per_step = CONV_RB // SUBLANES
    for lt in range(d // LANES):
        cols = slice(lt * LANES, (lt + 1) * LANES)
        wt = [dww_ref[j, :, cols] for j in range(CONV_SPAN)]
        bias = jnp.broadcast_to(dwb_ref[:, cols], (SUBLANES, LANES))

        def conv_body(i, carry, cols=cols, wt=wt, bias=bias):
            accs = [bias] * blocks_per_step
            for j in range(CONV_SPAN + CONV_RB - SUBLANES):
                src = shift + j
                row = ypad_ref[i * blocks_per_step + src // SUBLANES,
                               src % SUBLANES:src % SUBLANES + 1, cols]
                row = jnp.broadcast_to(row, (SUBLANES, LANES))
                for blk in range(blocks_per_step):
                    jj = j - blk * SUBLANES
                    if 0 <= jj < CONV_SPAN:
                        accs[blk] = accs[blk] + wt[jj] * row
            for blk in range(blocks_per_step):
                r0 = pl.multiple_of((i * blocks_per_step + blk) * SUBLANES, SUBLANES)
                conv_ref[pl.ds(r0, SUBLANES), cols] = accs[blk]
            return carry

        lax.fori_loop(0, ODD_ROWS // CONV_RB, conv_body, 0)
    ypad_ref[0:halo_blocks] = ypad_ref[tile_blocks:tile_blocks + halo_blocks]

    y = jax.nn.silu(_layer_norm(conv_ref[...], lng_ref[...], lnb_ref[...]))
    z = _dot(h, win_ref[:, 2 * d:3 * d])
    out = _dot((y * jax.nn.silu(z)).astype(BF16), wout_ref[...])
    o_ref[0] = x + _rms(out, gpost_ref[...])


def _odd_layer(x3, g_pre, w_in, dw_w, dw_b, ln_g, ln_b, w_out, g_post):
    batch, seq, d = x3.shape
    tile = lambda b, t: (b, t, 0)
    fixed = lambda b, t: (0, 0)
    vec = lambda v: v.reshape(1, d)
    tap = np.arange(CONV_SPAN)[:, None] - np.arange(SUBLANES)[None, :]
    dww = jnp.where(((tap >= 0) & (tap < CONV_K))[:, :, None],
                    dw_w[np.clip(tap, 0, CONV_K - 1)], 0.0)
    return pl.pallas_call(
        _odd_kernel,
        grid=(batch, seq // ODD_ROWS),
        in_specs=[pl.BlockSpec((1, ODD_ROWS, d), tile), pl.BlockSpec((1, d), fixed),
                  pl.BlockSpec((d, 3 * d), fixed),
                  pl.BlockSpec((CONV_SPAN, SUBLANES, d), lambda b, t: (0, 0, 0)),
                  pl.BlockSpec((1, d), fixed), pl.BlockSpec((1, d), fixed), pl.BlockSpec((1, d), fixed),
                  pl.BlockSpec((d, d), fixed), pl.BlockSpec((1, d), fixed)],
        out_specs=pl.BlockSpec((1, ODD_ROWS, d), tile),
        out_shape=jax.ShapeDtypeStruct((batch, seq, d), F32),
        scratch_shapes=[pltpu.VMEM(((CONV_HALO + ODD_ROWS) // SUBLANES, SUBLANES, d), F32),
                        pltpu.VMEM((ODD_ROWS, d), F32)],
        compiler_params=_cparams(("arbitrary", "arbitrary")),
        name="odd_layer",
    )(x3, vec(g_pre), w_in.astype(BF16), dww, vec(dw_b), vec(ln_g), vec(ln_b),
      w_out.astype(BF16), vec(g_post))


def kernel(x, norm_pre, norm_post, even_w_in, even_cmp_k_pe, even_cmp_k_w1, even_cmp_k_w2,
           even_cmp_v_pe, even_cmp_v_w1, even_cmp_v_w2, even_sgu_ln_g, even_sgu_ln_b,
           even_sgu_w, even_sgu_b, even_w_out, odd_w_in, odd_dw_w, odd_dw_b, odd_ln_g,
           odd_ln_b, odd_w_out):
    batch, seq, d = x.shape
    depth = norm_pre.shape[0]
    for i in range(depth):
        li = i // 2
        if i % 2 == 0:
            x2 = x.reshape(batch * seq, d)
            q, kcvc, kvsw, sa, usb, vn, gates = _even_in(
                x2, norm_pre[i], even_w_in[li], even_sgu_ln_g[li], even_sgu_ln_b[li])
            cw = _compress_weights(even_cmp_k_pe[li], even_cmp_k_w1[li], even_cmp_k_w2[li],
                                   even_cmp_v_pe[li], even_cmp_v_w1[li], even_cmp_v_w2[li])
            kcmp, vcmp = _compress(kcvc, batch, seq, cw)
            a = _attention(q, kvsw, kcmp, vcmp, gates, batch, seq)
            x = _even_out(a.reshape(batch * seq, NSA_W), sa, usb, vn, even_sgu_w[li], even_sgu_b[li],
                          even_w_out[li], norm_post[i], x2).reshape(batch, seq, d)
        else:
            x = _odd_layer(x, norm_pre[i], odd_w_in[li], odd_dw_w[li], odd_dw_b[li],
                           odd_ln_g[li], odd_ln_b[li], odd_w_out[li], norm_post[i])
    return x
```

```python
import functools

import jax
import jax.numpy as jnp
import numpy as np
from jax import lax
from jax.experimental import pallas as pl
from jax.experimental.pallas import tpu as pltpu

N_HEADS = 8
N_KV = 2
GROUP = N_HEADS // N_KV
HEAD_DIM = 64
NSA_W = N_HEADS * HEAD_DIM
KV_W = N_KV * HEAD_DIM
CMP_BLOCK = 32
CMP_STRIDE = 16
SLC_BLOCK = 64
SLC_TOPK = 8
WINDOW = 512
FORCE_BONUS = 1e4
NEG_INF = -1e30
SGU_GROUPS = 8
SGU_GROUP_DIM = 64
SGU_W = SGU_GROUPS * SGU_GROUP_DIM
SGU_CHUNK = 128
CONV_K = 31
RMS_EPS = 1e-6
LN_EPS = 1e-5

LANES = 128
SUBLANES = 8
VMEM_LIMIT_BYTES = 56 * 1024 * 1024

IN_ROWS = 512
ATT_Q = 128
ATT_PAST = 512
OUT_ROWS = 256
ODD_ROWS = 512
CONV_HALO = 32
CONV_RB = 16
CONV_SPAN = CONV_K + SUBLANES - 1

F32 = jnp.float32
BF16 = jnp.bfloat16


def _cparams(sem):
    return pltpu.CompilerParams(dimension_semantics=sem, vmem_limit_bytes=VMEM_LIMIT_BYTES)


def _rms(x, g):
    return x * lax.rsqrt(jnp.mean(x * x, axis=-1, keepdims=True) + RMS_EPS) * g


def _layer_norm(x, g, b):
    mu = jnp.mean(x, axis=-1, keepdims=True)
    xc = x - mu
    return xc * lax.rsqrt(jnp.mean(xc * xc, axis=-1, keepdims=True) + LN_EPS) * g + b


def _dot(a, b):
    return jnp.dot(a, b, preferred_element_type=F32)


def _dot_nt(a, b):
    return lax.dot_general(a, b, (((1,), (1,)), ((), ())), preferred_element_type=F32)


EV_Q, EV_KC, EV_KS, EV_ZA, EV_U, EV_V, EV_ZB, EV_G, EV_END = 0, 512, 768, 1280, 1792, 2304, 2816, 3328, 3456


def _even_in_kernel(x_ref, g_ref, w_ref, lng_ref, lnb_ref,
                    q_ref, kcvc_ref, kvsw_ref, sa_ref, usb_ref, vn_ref, gate_ref):
    h = _rms(x_ref[...], g_ref[...]).astype(BF16)
    q_ref[...] = (_dot(h, w_ref[:, EV_Q:EV_KC]) * (HEAD_DIM ** -0.5)).astype(BF16)
    kcvc_ref[...] = _dot(h, w_ref[:, EV_KC:EV_KS]).astype(BF16)
    kvsw_ref[...] = _dot(h, w_ref[:, EV_KS:EV_ZA]).astype(BF16)
    sa_ref[...] = jax.nn.silu(_dot(h, w_ref[:, EV_ZA:EV_U])).astype(BF16)
    u = _dot(h, w_ref[:, EV_U:EV_V])
    zb = _dot(h, w_ref[:, EV_ZB:EV_G])
    usb_ref[...] = (jax.nn.gelu(u) * jax.nn.silu(zb)).astype(BF16)
    v = jax.nn.gelu(_dot(h, w_ref[:, EV_V:EV_ZB]))
    vn_ref[...] = _layer_norm(v, lng_ref[...], lnb_ref[...]).astype(BF16)
    gate_ref[...] = jax.nn.sigmoid(_dot(h, w_ref[:, EV_G:EV_END]))


def _even_in(x2, g_pre, w_in, ln_g, ln_b):
    t, d = x2.shape
    w = jnp.concatenate(
        [w_in[:, :1280], w_in[:, 1304:], w_in[:, 1280:1304],
         jnp.zeros((d, LANES - 3 * N_HEADS), w_in.dtype)], axis=1).astype(BF16)
    row = lambda i: (i, 0)
    fixed = lambda i: (0, 0)
    outs = [(NSA_W, BF16), (2 * KV_W, BF16), (4 * KV_W, BF16), (NSA_W, BF16),
            (SGU_W, BF16), (SGU_W, BF16), (LANES, F32)]
    return pl.pallas_call(
        _even_in_kernel,
        grid=(t // IN_ROWS,),
        in_specs=[pl.BlockSpec((IN_ROWS, d), row), pl.BlockSpec((1, d), fixed),
                  pl.BlockSpec((d, EV_END), fixed), pl.BlockSpec((1, SGU_W), fixed),
                  pl.BlockSpec((1, SGU_W), fixed)],
        out_specs=[pl.BlockSpec((IN_ROWS, n), row) for n, _ in outs],
        out_shape=[jax.ShapeDtypeStruct((t, n), dt) for n, dt in outs],
        compiler_params=_cparams(("arbitrary",)),
        name="even_in",
    )(x2, g_pre.reshape(1, d), w, ln_g.reshape(1, SGU_W), ln_b.reshape(1, SGU_W))


def _compress_kernel(kr_ref, pea_ref, peb_ref, w1a_ref, w1b_ref, w2_ref, kcmp_ref, vcmp_ref):
    kr = kr_ref[0].astype(F32)
    n_rows = kr.shape[0]
    first = _dot((kr + pea_ref[...]).astype(BF16), w1a_ref[...])
    second = _dot((kr + peb_ref[...]).astype(BF16), w1b_ref[...])
    pre = first + pltpu.roll(second, n_rows - 1, 0)
    out = _dot(jax.nn.silu(pre).astype(BF16), w2_ref[...])
    rows = lax.broadcasted_iota(jnp.int32, out.shape, 0)
    out = jnp.where(rows < n_rows - 1, out, 0.0)
    kcmp_ref[0] = out[:, :KV_W].astype(BF16)
    vcmp_ref[0] = out[:, KV_W:].astype(BF16)


def _compress_weights(k_pe, k_w1, k_w2, v_pe, v_w1, v_w2):
    dh, half = HEAD_DIM, CMP_STRIDE
    eye_g = jnp.eye(N_KV, dtype=F32)

    def first_layer(w1, lo):
        w = w1.reshape(CMP_BLOCK, dh, dh)[lo:lo + half]
        return jnp.einsum('lde,gh->lgdhe', w, eye_g).reshape(half, KV_W, KV_W)

    def both(lo):
        wk, wv = first_layer(k_w1, lo), first_layer(v_w1, lo)
        z = jnp.zeros_like(wk)
        top = jnp.concatenate([wk, z], axis=2)
        bot = jnp.concatenate([z, wv], axis=2)
        return jnp.concatenate([top, bot], axis=1).reshape(half * 2 * KV_W, 2 * KV_W).astype(BF16)

    def pe_row(lo):
        pk = jnp.tile(k_pe[lo:lo + half], (1, N_KV))
        pv = jnp.tile(v_pe[lo:lo + half], (1, N_KV))
        return jnp.concatenate([pk, pv], axis=1).reshape(1, half * 2 * KV_W)

    w2 = jnp.zeros((2 * KV_W, 2 * KV_W), F32)
    for i, w in enumerate((k_w2, k_w2, v_w2, v_w2)):
        w2 = w2.at[i * dh:(i + 1) * dh, i * dh:(i + 1) * dh].set(w)
    return pe_row(0), pe_row(half), both(0), both(half), w2.astype(BF16)


def _compress(kcvc, batch, seq, cw):
    pea, peb, w1a, w1b, w2 = cw
    n_rows = seq // CMP_STRIDE
    width = CMP_STRIDE * 2 * KV_W
    kr = kcvc.reshape(batch, n_rows, width)
    fixed = lambda b: (0, 0)
    per_b = lambda b: (b, 0, 0)
    return pl.pallas_call(
        _compress_kernel,
        grid=(batch,),
        in_specs=[pl.BlockSpec((1, n_rows, width), per_b),
                  pl.BlockSpec((1, width), fixed), pl.BlockSpec((1, width), fixed),
                  pl.BlockSpec((width, 2 * KV_W), fixed), pl.BlockSpec((width, 2 * KV_W), fixed),
                  pl.BlockSpec((2 * KV_W, 2 * KV_W), fixed)],
        out_specs=[pl.BlockSpec((1, n_rows, KV_W), per_b)] * 2,
        out_shape=[jax.ShapeDtypeStruct((batch, n_rows, KV_W), BF16)] * 2,
        compiler_params=_cparams(("arbitrary",)),
        name="compress",
    )(kr, pea, peb, w1a, w1b, w2)


def _overlap_matrix(n_rows, n_blk):
    n_cmp = n_rows - 1
    tok = np.arange(n_cmp)[:, None] * CMP_STRIDE + np.arange(CMP_BLOCK)[None, :]
    ov = ((tok[:, :, None] // SLC_BLOCK) == np.arange(n_blk)[None, None, :]).sum(1)
    ov = ov.astype(np.float32) / np.float32(CMP_BLOCK)
    out = np.zeros((N_KV, LANES, n_rows), np.float32)
    for g in range(N_KV):
        other = (1 - g) * HEAD_DIM
        out[g, other:other + n_blk, :n_cmp] = ov.T
    return out


def _attn_kernel(q_ref, kv_ref, kcmp_ref, vcmp_ref, gate_ref, ovt_ref, o_ref,
                 ksel_ref, kwin_ref, vwin_ref, osel_ref, *, seq):
    tq = ATT_Q
    n_blk = seq // SLC_BLOCK
    t = pl.program_id(1)
    t0 = pl.multiple_of(t * tq, tq)
    lane = lax.broadcasted_iota(jnp.int32, (tq, LANES), 1)
    tok4 = t0 + (lax.broadcasted_iota(jnp.int32, (GROUP * tq, 1), 0) & (tq - 1))
    row4 = lax.broadcasted_iota(jnp.int32, (GROUP * tq, LANES), 0) & (tq - 1)
    lane4 = lax.broadcasted_iota(jnp.int32, (GROUP * tq, LANES), 1)
    causal = lane4 <= row4
    gates = gate_ref[...]
    zero_bf = jnp.zeros((), BF16)

    @pl.when(t == 0)
    def _build_keys():
        key_blk = lax.broadcasted_iota(jnp.int32, (seq, LANES), 0) // SLC_BLOCK
        lane_s = lax.broadcasted_iota(jnp.int32, (seq, LANES), 1)
        lane_w = lax.broadcasted_iota(jnp.int32, (WINDOW, LANES), 1)
        for g in range(N_KV):
            other = (1 - g) * HEAD_DIM
            mine = (lane_s // HEAD_DIM) == g
            onehot = jnp.where(lane_s - other == key_blk, 1.0, 0.0).astype(BF16)
            ksel_ref[g] = jnp.where(mine, kv_ref[0, :, 0:KV_W], onehot)
            kwin_ref[g, WINDOW:, :] = jnp.where(mine, kv_ref[0, :, 2 * KV_W:3 * KV_W], zero_bf)
            kwin_ref[g, 0:WINDOW, :] = jnp.where(lane_w == other, 1.0, 0.0).astype(BF16)
        vwin_ref[0:WINDOW, :] = jnp.zeros((WINDOW, LANES), BF16)
        vwin_ref[WINDOW:, :] = kv_ref[0, :, 3 * KV_W:4 * KV_W]

    out_blocks = [None] * (N_HEADS // 2)
    for g in range(N_KV):
        other = (1 - g) * HEAD_DIM
        in_half = (lane // HEAD_DIM) == g
        q32 = []
        for r in range(GROUP):
            h = g * GROUP + r
            blk = q_ref[0, :, (h // 2) * LANES:(h // 2 + 1) * LANES].astype(F32)
            if h % 2 != g:
                blk = pltpu.roll(blk, HEAD_DIM, 1)
            q32.append(blk)

        def stack(fill):
            return jnp.concatenate([jnp.where(in_half, qh, fill).astype(BF16) for qh in q32], axis=0)

        qg = stack(0.0)

        s = _dot_nt(qg, kcmp_ref[0])
        n_idx = lax.broadcasted_iota(jnp.int32, s.shape, 1)
        valid = n_idx * CMP_STRIDE + (CMP_BLOCK - 1) <= tok4
        s = jnp.where(valid, s, NEG_INF)
        p = jnp.exp(s - jnp.max(s, axis=-1, keepdims=True))
        p = jnp.where(valid, p / jnp.sum(p, axis=-1, keepdims=True), 0.0)
        o_cmp = _dot(p.astype(BF16), vcmp_ref[0])

        psum = p[0:tq] + p[tq:2 * tq] + p[2 * tq:3 * tq] + p[3 * tq:4 * tq]
        imp_t = lax.dot_general(ovt_ref[g], psum, (((1,), (1,)), ((), ())),
                                preferred_element_type=F32, precision=lax.Precision.HIGHEST)
        imp = imp_t[other:other + n_blk, :]
        blk_id = lax.broadcasted_iota(jnp.int32, (n_blk, tq), 0)
        cur = (t0 + lax.broadcasted_iota(jnp.int32, (n_blk, tq), 1)) // SLC_BLOCK
        forced = (blk_id == 0) | (blk_id == cur) | (blk_id == cur - 1)
        imp = jnp.where(blk_id <= cur, imp + jnp.where(forced, FORCE_BONUS, 0.0), NEG_INF)
        rank = jnp.zeros((n_blk, tq), F32)
        for j in range(n_blk):
            other_blk = imp[j:j + 1, :]
            ahead = (other_blk > imp) | ((other_blk == imp) & (blk_id > j))
            rank = rank + jnp.where(ahead, 1.0, 0.0)
        keep = (rank < SLC_TOPK) & (blk_id < t * (tq // SLC_BLOCK))
        pen = jnp.where(keep, 0.0, NEG_INF)
        pen = jnp.concatenate([jnp.zeros((n, tq), F32) for n in (other,) if n] + [pen]
                              + [jnp.zeros((n, tq), F32) for n in (LANES - other - n_blk,) if n], axis=0)
        qg_sel = stack(pen.T)

        s_d = jnp.where(causal, _dot_nt(qg, ksel_ref[g, pl.ds(t0, tq), :]), NEG_INF)
        m_d = jnp.max(s_d, axis=-1, keepdims=True)
        v_d = kv_ref[0, pl.ds(t0, tq), KV_W:2 * KV_W]
        n_past = (t * tq + ATT_PAST - 1) // ATT_PAST

        @pl.when(n_past == 0)
        def _():
            p_d = jnp.exp(s_d - m_d)
            osel_ref[...] = _dot(p_d.astype(BF16), v_d) / jnp.sum(p_d, axis=-1, keepdims=True)

        for nb in range(1, seq // ATT_PAST + 1):
            @pl.when(n_past == nb)
            def _(nb=nb):
                s_p = _dot_nt(qg_sel, ksel_ref[g, 0:nb * ATT_PAST, :])
                m = jnp.maximum(m_d, jnp.max(s_p, axis=-1, keepdims=True))
                p_d = jnp.exp(s_d - m)
                p_p = jnp.exp(s_p - m)
                l = jnp.sum(p_d, axis=-1, keepdims=True) + jnp.sum(p_p, axis=-1, keepdims=True)
                acc = (_dot(p_d.astype(BF16), v_d)
                       + _dot(p_p.astype(BF16), kv_ref[0, 0:nb * ATT_PAST, KV_W:2 * KV_W]))
                osel_ref[...] = acc / l

        o_sel = osel_ref[...]

        qg_win = stack(jnp.where(lane == other, NEG_INF, 0.0))
        sw = _dot_nt(qg_win, kwin_ref[g, pl.ds(t0, WINDOW + tq), :])
        sw = jnp.concatenate([jnp.where(lane4 > row4, sw[:, 0:tq], NEG_INF), sw[:, tq:WINDOW],
                              jnp.where(causal, sw[:, WINDOW:], NEG_INF)], axis=1)
        pw = jnp.exp(sw - jnp.max(sw, axis=-1, keepdims=True))
        o_win = (_dot(pw.astype(BF16), vwin_ref[pl.ds(t0, WINDOW + tq), :])
                 / jnp.sum(pw, axis=-1, keepdims=True))

        for r in range(GROUP):
            h = g * GROUP + r
            rows = slice(r * tq, (r + 1) * tq)
            mixed = (gates[:, h:h + 1] * o_cmp[rows]
                     + gates[:, N_HEADS + h:N_HEADS + h + 1] * o_sel[rows]
                     + gates[:, 2 * N_HEADS + h:2 * N_HEADS + h + 1] * o_win[rows])
            if h % 2 != g:
                mixed = pltpu.roll(mixed, HEAD_DIM, 1)
            keep = (lane // HEAD_DIM) == (h % 2)
            prev = out_blocks[h // 2]
            out_blocks[h // 2] = jnp.where(keep, mixed, 0.0 if prev is None else prev)

    for i, blk in enumerate(out_blocks):
        o_ref[0, :, i * LANES:(i + 1) * LANES] = blk.astype(BF16)


def _attention(q, kvsw, kcmp, vcmp, gates, batch, seq):
    n_rows = seq // CMP_STRIDE
    ovt = jnp.asarray(_overlap_matrix(n_rows, seq // SLC_BLOCK))
    tile = lambda b, t: (b, t, 0)
    per_b = lambda b, t: (b, 0, 0)
    return pl.pallas_call(
        functools.partial(_attn_kernel, seq=seq),
        grid=(batch, seq // ATT_Q),
        in_specs=[pl.BlockSpec((1, ATT_Q, NSA_W), tile),
                  pl.BlockSpec((1, seq, 4 * KV_W), per_b),
                  pl.BlockSpec((1, n_rows, KV_W), per_b),
                  pl.BlockSpec((1, n_rows, KV_W), per_b),
                  pl.BlockSpec((ATT_Q, LANES), lambda b, t: (b * (seq // ATT_Q) + t, 0)),
                  pl.BlockSpec((N_KV, LANES, n_rows), lambda b, t: (0, 0, 0))],
        out_specs=pl.BlockSpec((1, ATT_Q, NSA_W), tile),
        out_shape=jax.ShapeDtypeStruct((batch, seq, NSA_W), BF16),
        scratch_shapes=[pltpu.VMEM((N_KV, seq, LANES), BF16),
                        pltpu.VMEM((N_KV, WINDOW + seq, LANES), BF16),
                        pltpu.VMEM((WINDOW + seq, LANES), BF16),
                        pltpu.VMEM((GROUP * ATT_Q, LANES), F32)],
        compiler_params=_cparams(("arbitrary", "arbitrary")),
        name="nsa_attention",
    )(q.reshape(batch, seq, NSA_W), kvsw.reshape(batch, seq, 4 * KV_W), kcmp, vcmp, gates, ovt)


def _even_out_kernel(a_ref, sa_ref, usb_ref, vn_ref, ws_ref, bias_ref, w_ref, g_ref, x_ref, o_ref):
    lane = lax.broadcasted_iota(jnp.int32, (SGU_CHUNK, LANES), 1)
    low = lane < SGU_GROUP_DIM
    ri = lax.broadcasted_iota(jnp.int32, (SGU_CHUNK, SGU_CHUNK), 0)
    ci = lax.broadcasted_iota(jnp.int32, (SGU_CHUNK, SGU_CHUNK), 1)
    tril = ci <= ri
    wmix = [jnp.where(tril, ws_ref[g], 0.0).astype(BF16) for g in range(SGU_GROUPS)]
    zero = jnp.zeros((), BF16)
    left = (a_ref[...].astype(F32) * sa_ref[...].astype(F32)).astype(BF16)
    rights = []
    for c in range(OUT_ROWS // SGU_CHUNK):
        rows = slice(c * SGU_CHUNK, (c + 1) * SGU_CHUNK)
        blocks = []
        for p in range(SGU_W // LANES):
            cols = slice(p * LANES, (p + 1) * LANES)
            vb = vn_ref[rows, cols]
            mixed = (_dot(wmix[2 * p], jnp.where(low, vb, zero))
                     + _dot(wmix[2 * p + 1], jnp.where(low, zero, vb))
                     + bias_ref[:, cols])
            blocks.append((usb_ref[rows, cols].astype(F32) * mixed).astype(BF16))
        rights.append(jnp.concatenate(blocks, axis=1))
    right = jnp.concatenate(rights, axis=0)
    y = _dot(left, w_ref[0:NSA_W, :]) + _dot(right, w_ref[NSA_W:, :])
    o_ref[...] = x_ref[...] + _rms(y, g_ref[...])


def _even_out(a, sa, usb, vn, sgu_w, sgu_b, w_out, g_post, x2):
    t, d = x2.shape
    bias = jnp.repeat(sgu_b.T, SGU_GROUP_DIM, axis=1)
    row = lambda i: (i, 0)
    fixed = lambda i: (0, 0)
    return pl.pallas_call(
        _even_out_kernel,
        grid=(t // OUT_ROWS,),
        in_specs=[pl.BlockSpec((OUT_ROWS, NSA_W), row), pl.BlockSpec((OUT_ROWS, NSA_W), row),
                  pl.BlockSpec((OUT_ROWS, SGU_W), row), pl.BlockSpec((OUT_ROWS, SGU_W), row),
                  pl.BlockSpec((SGU_GROUPS, SGU_CHUNK, SGU_CHUNK), lambda i: (0, 0, 0)),
                  pl.BlockSpec((SGU_CHUNK, SGU_W), fixed),
                  pl.BlockSpec((NSA_W + SGU_W, d), fixed), pl.BlockSpec((1, d), fixed),
                  pl.BlockSpec((OUT_ROWS, d), row)],
        out_specs=pl.BlockSpec((OUT_ROWS, d), row),
        out_shape=jax.ShapeDtypeStruct((t, d), F32),
        compiler_params=_cparams(("arbitrary",)),
        name="even_out",
    )(a, sa, usb, vn, sgu_w, bias, w_out.astype(BF16), g_post.reshape(1, d), x2)


def _odd_kernel(x_ref, gpre_ref, win_ref, dww_ref, dwb_ref, lng_ref, lnb_ref, wout_ref, gpost_ref,
                o_ref, ypad_ref, conv_ref):
    d = x_ref.shape[-1]
    x = x_ref[0]
    h = _rms(x, gpre_ref[...]).astype(BF16)
    a = _dot(h, win_ref[:, 0:d])
    gl = _dot(h, win_ref[:, d:2 * d])

    halo_blocks = CONV_HALO // SUBLANES
    tile_blocks = ODD_ROWS // SUBLANES

    @pl.when(pl.program_id(1) == 0)
    def _():
        ypad_ref[0:halo_blocks] = jnp.zeros((halo_blocks, SUBLANES, d), F32)

    ypad_ref[halo_blocks:] = (a * jax.nn.sigmoid(gl)).reshape(tile_blocks, SUBLANES, d)

    shift = CONV_HALO - (CONV_K - 1)
    blocks_per_step = CONV_RB // SUBLANES
    for lt in range(d // LANES):
        cols = slice(lt * LANES, (lt + 1) * LANES)
        wt = [dww_ref[j, :, cols] for j in range(CONV_SPAN)]
        bias = jnp.broadcast_to(dwb_ref[:, cols], (SUBLANES, LANES))

        def conv_body(i, carry, cols=cols, wt=wt, bias=bias):
            accs = [bias] * blocks_per_step
            for j in range(CONV_SPAN + CONV_RB - SUBLANES):
                src = shift + j
                row = ypad_ref[i * blocks_per_step + src // SUBLANES,
                               src % SUBLANES:src % SUBLANES + 1, cols]
                row = jnp.broadcast_to(row, (SUBLANES, LANES))
                for blk in range(blocks_per_step):
                    jj = j - blk * SUBLANES
                    if 0 <= jj < CONV_SPAN:
                        accs[blk] = accs[blk] + wt[jj] * row
            for blk in range(blocks_per_step):
                r0 = pl.multiple_of((i * blocks_per_step + blk) * SUBLANES, SUBLANES)
                conv_ref[pl.ds(r0, SUBLANES), cols] = accs[blk]
            return carry

        lax.fori_loop(0, ODD_ROWS // CONV_RB, conv_body, 0)
    ypad_ref[0:halo_blocks] = ypad_ref[tile_blocks:tile_blocks + halo_blocks]

    y = jax.nn.silu(_layer_norm(conv_ref[...], lng_ref[...], lnb_ref[...]))
    z = _dot(h, win_ref[:, 2 * d:3 * d])
    out = _dot((y * jax.nn.silu(z)).astype(BF16), wout_ref[...])
    o_ref[0] = x + _rms(out, gpost_ref[...])


def _odd_layer(x3, g_pre, w_in, dw_w, dw_b, ln_g, ln_b, w_out, g_post):
    batch, seq, d = x3.shape
    tile = lambda b, t: (b, t, 0)
    fixed = lambda b, t: (0, 0)
    vec = lambda v: v.reshape(1, d)
    tap = np.arange(CONV_SPAN)[:, None] - np.arange(SUBLANES)[None, :]
    dww = jnp.where(((tap >= 0) & (tap < CONV_K))[:, :, None],
                    dw_w[np.clip(tap, 0, CONV_K - 1)], 0.0)
    return pl.pallas_call(
        _odd_kernel,
        grid=(batch, seq // ODD_ROWS),
        in_specs=[pl.BlockSpec((1, ODD_ROWS, d), tile), pl.BlockSpec((1, d), fixed),
                  pl.BlockSpec((d, 3 * d), fixed),
                  pl.BlockSpec((CONV_SPAN, SUBLANES, d), lambda b, t: (0, 0, 0)),
                  pl.BlockSpec((1, d), fixed), pl.BlockSpec((1, d), fixed), pl.BlockSpec((1, d), fixed),
                  pl.BlockSpec((d, d), fixed), pl.BlockSpec((1, d), fixed)],
        out_specs=pl.BlockSpec((1, ODD_ROWS, d), tile),
        out_shape=jax.ShapeDtypeStruct((batch, seq, d), F32),
        scratch_shapes=[pltpu.VMEM(((CONV_HALO + ODD_ROWS) // SUBLANES, SUBLANES, d), F32),
                        pltpu.VMEM((ODD_ROWS, d), F32)],
        compiler_params=_cparams(("arbitrary", "arbitrary")),
        name="odd_layer",
    )(x3, vec(g_pre), w_in.astype(BF16), dww, vec(dw_b), vec(ln_g), vec(ln_b),
      w_out.astype(BF16), vec(g_post))


def kernel(x, norm_pre, norm_post, even_w_in, even_cmp_k_pe, even_cmp_k_w1, even_cmp_k_w2,
           even_cmp_v_pe, even_cmp_v_w1, even_cmp_v_w2, even_sgu_ln_g, even_sgu_ln_b,
           even_sgu_w, even_sgu_b, even_w_out, odd_w_in, odd_dw_w, odd_dw_b, odd_ln_g,
           odd_ln_b, odd_w_out):
    batch, seq, d = x.shape
    depth = norm_pre.shape[0]
    for i in range(depth):
        li = i // 2
        if i % 2 == 0:
            x2 = x.reshape(batch * seq, d)
            q, kcvc, kvsw, sa, usb, vn, gates = _even_in(
                x2, norm_pre[i], even_w_in[li], even_sgu_ln_g[li], even_sgu_ln_b[li])
            cw = _compress_weights(even_cmp_k_pe[li], even_cmp_k_w1[li], even_cmp_k_w2[li],
                                   even_cmp_v_pe[li], even_cmp_v_w1[li], even_cmp_v_w2[li])
            kcmp, vcmp = _compress(kcvc, batch, seq, cw)
            a = _attention(q, kvsw, kcmp, vcmp, gates, batch, seq)
            x = _even_out(a.reshape(batch * seq, NSA_W), sa, usb, vn, even_sgu_w[li], even_sgu_b[li],
                          even_w_out[li], norm_post[i], x2).reshape(batch, seq, d)
        else:
            x = _odd_layer(x, norm_pre[i], odd_w_in[li], odd_dw_w[li], odd_dw_b[li],
                           odd_ln_g[li], odd_ln_b[li], odd_w_out[li], norm_post[i])
    return x
```

```python
import functools

import jax
import jax.numpy as jnp
import numpy as np
from jax import lax
from jax.experimental import pallas as pl
from jax.experimental.pallas import tpu as pltpu

N_HEADS = 8
N_KV = 2
GROUP = N_HEADS // N_KV
HEAD_DIM = 64
NSA_W = N_HEADS * HEAD_DIM
KV_W = N_KV * HEAD_DIM
CMP_BLOCK = 32
CMP_STRIDE = 16
SLC_BLOCK = 64
SLC_TOPK = 8
WINDOW = 512
FORCE_BONUS = 1e4
NEG_INF = -1e30
SGU_GROUPS = 8
SGU_GROUP_DIM = 64
SGU_W = SGU_GROUPS * SGU_GROUP_DIM
SGU_CHUNK = 128
CONV_K = 31
RMS_EPS = 1e-6
LN_EPS = 1e-5

LANES = 128
SUBLANES = 8
VMEM_LIMIT_BYTES = 56 * 1024 * 1024

IN_ROWS = 512
ATT_Q = 128
ATT_PAST = 256
OUT_ROWS = 256
ODD_ROWS = 512
CONV_HALO = 32
CONV_RB = 32
CONV_CHAINS = 2
CONV_SPAN = CONV_K + SUBLANES - 1

F32 = jnp.float32
BF16 = jnp.bfloat16


def _cparams(sem):
    return pltpu.CompilerParams(dimension_semantics=sem, vmem_limit_bytes=VMEM_LIMIT_BYTES)


def _rms(x, g):
    return x * lax.rsqrt(jnp.mean(x * x, axis=-1, keepdims=True) + RMS_EPS) * g


def _layer_norm(x, g, b):
    mu = jnp.mean(x, axis=-1, keepdims=True)
    xc = x - mu
    return xc * lax.rsqrt(jnp.mean(xc * xc, axis=-1, keepdims=True) + LN_EPS) * g + b


def _dot(a, b):
    return jnp.dot(a, b, preferred_element_type=F32)


EV_Q, EV_KC, EV_KS, EV_ZA, EV_U, EV_V, EV_ZB, EV_G, EV_END = 0, 512, 768, 1280, 1792, 2304, 2816, 3328, 3456


def _even_in_kernel(x_ref, g_ref, w_ref, lng_ref, lnb_ref,
                    q_ref, kcvc_ref, kvsw_ref, sa_ref, usb_ref, vn_ref, gate_ref):
    h = _rms(x_ref[...], g_ref[...]).astype(BF16)
    q_ref[...] = (_dot(h, w_ref[:, EV_Q:EV_KC]) * (HEAD_DIM ** -0.5)).astype(BF16)
    kcvc_ref[...] = _dot(h, w_ref[:, EV_KC:EV_KS]).astype(BF16)
    kvsw_ref[...] = _dot(h, w_ref[:, EV_KS:EV_ZA]).astype(BF16)
    sa_ref[...] = jax.nn.silu(_dot(h, w_ref[:, EV_ZA:EV_U])).astype(BF16)
    u = _dot(h, w_ref[:, EV_U:EV_V])
    zb = _dot(h, w_ref[:, EV_ZB:EV_G])
    usb_ref[...] = (jax.nn.gelu(u) * jax.nn.silu(zb)).astype(BF16)
    v = jax.nn.gelu(_dot(h, w_ref[:, EV_V:EV_ZB]))
    vn_ref[...] = _layer_norm(v, lng_ref[...], lnb_ref[...]).astype(BF16)
    gate_ref[...] = jax.nn.sigmoid(_dot(h, w_ref[:, EV_G:EV_END]))


def _even_in(x2, g_pre, w_in, ln_g, ln_b):
    t, d = x2.shape
    w = jnp.concatenate(
        [w_in[:, :1280], w_in[:, 1304:], w_in[:, 1280:1304],
         jnp.zeros((d, LANES - 3 * N_HEADS), w_in.dtype)], axis=1).astype(BF16)
    row = lambda i: (i, 0)
    fixed = lambda i: (0, 0)
    outs = [(NSA_W, BF16), (2 * KV_W, BF16), (4 * KV_W, BF16), (NSA_W, BF16),
            (SGU_W, BF16), (SGU_W, BF16), (LANES, F32)]
    return pl.pallas_call(
        _even_in_kernel,
        grid=(t // IN_ROWS,),
        in_specs=[pl.BlockSpec((IN_ROWS, d), row), pl.BlockSpec((1, d), fixed),
                  pl.BlockSpec((d, EV_END), fixed), pl.BlockSpec((1, SGU_W), fixed),
                  pl.BlockSpec((1, SGU_W), fixed)],
        out_specs=[pl.BlockSpec((IN_ROWS, n), row) for n, _ in outs],
        out_shape=[jax.ShapeDtypeStruct((t, n), dt) for n, dt in outs],
        compiler_params=_cparams(("arbitrary",)),
        name="even_in",
    )(x2, g_pre.reshape(1, d), w, ln_g.reshape(1, SGU_W), ln_b.reshape(1, SGU_W))


def _compress_kernel(kr_ref, pea_ref, peb_ref, w1a_ref, w1b_ref, w2_ref, kcmp_ref, vcmp_ref):
    kr = kr_ref[0].astype(F32)
    n_rows = kr.shape[0]
    first = _dot((kr + pea_ref[...]).astype(BF16), w1a_ref[...])
    second = _dot((kr + peb_ref[...]).astype(BF16), w1b_ref[...])
    pre = first + pltpu.roll(second, n_rows - 1, 0)
    out = _dot(jax.nn.silu(pre).astype(BF16), w2_ref[...])
    rows = lax.broadcasted_iota(jnp.int32, out.shape, 0)
    out = jnp.where(rows < n_rows - 1, out, 0.0)
    kcmp_ref[0] = out[:, :KV_W].astype(BF16)
    vcmp_ref[0] = out[:, KV_W:].astype(BF16)


def _compress_weights(k_pe, k_w1, k_w2, v_pe, v_w1, v_w2):
    dh, half = HEAD_DIM, CMP_STRIDE
    eye_g = jnp.eye(N_KV, dtype=F32)

    def first_layer(w1, lo):
        w = w1.reshape(CMP_BLOCK, dh, dh)[lo:lo + half]
        return jnp.einsum('lde,gh->lgdhe', w, eye_g).reshape(half, KV_W, KV_W)

    def both(lo):
        wk, wv = first_layer(k_w1, lo), first_layer(v_w1, lo)
        z = jnp.zeros_like(wk)
        top = jnp.concatenate([wk, z], axis=2)
        bot = jnp.concatenate([z, wv], axis=2)
        return jnp.concatenate([top, bot], axis=1).reshape(half * 2 * KV_W, 2 * KV_W).astype(BF16)

    def pe_row(lo):
        pk = jnp.tile(k_pe[lo:lo + half], (1, N_KV))
        pv = jnp.tile(v_pe[lo:lo + half], (1, N_KV))
        return jnp.concatenate([pk, pv], axis=1).reshape(1, half * 2 * KV_W)

    w2 = jnp.zeros((2 * KV_W, 2 * KV_W), F32)
    for i, w in enumerate((k_w2, k_w2, v_w2, v_w2)):
        w2 = w2.at[i * dh:(i + 1) * dh, i * dh:(i + 1) * dh].set(w)
    return pe_row(0), pe_row(half), both(0), both(half), w2.astype(BF16)


def _compress(kcvc, batch, seq, cw):
    pea, peb, w1a, w1b, w2 = cw
    n_rows = seq // CMP_STRIDE
    width = CMP_STRIDE * 2 * KV_W
    kr = kcvc.reshape(batch, n_rows, width)
    fixed = lambda b: (0, 0)
    per_b = lambda b: (b, 0, 0)
    return pl.pallas_call(
        _compress_kernel,
        grid=(batch,),
        in_specs=[pl.BlockSpec((1, n_rows, width), per_b),
                  pl.BlockSpec((1, width), fixed), pl.BlockSpec((1, width), fixed),
                  pl.BlockSpec((width, 2 * KV_W), fixed), pl.BlockSpec((width, 2 * KV_W), fixed),
                  pl.BlockSpec((2 * KV_W, 2 * KV_W), fixed)],
        out_specs=[pl.BlockSpec((1, n_rows, KV_W), per_b)] * 2,
        out_shape=[jax.ShapeDtypeStruct((batch, n_rows, KV_W), BF16)] * 2,
        compiler_params=_cparams(("arbitrary",)),
        name="compress",
    )(kr, pea, peb, w1a, w1b, w2)


def _overlap_matrix(n_rows, n_blk):
    n_cmp = n_rows - 1
    tok = np.arange(n_cmp)[:, None] * CMP_STRIDE + np.arange(CMP_BLOCK)[None, :]
    ov = ((tok[:, :, None] // SLC_BLOCK) == np.arange(n_blk)[None, None, :]).sum(1)
    ov = ov.astype(np.float32) / np.float32(CMP_BLOCK)
    out = np.zeros((N_KV, LANES, n_rows), np.float32)
    for g in range(N_KV):
        other = (1 - g) * HEAD_DIM
        out[g, other:other + n_blk, :n_cmp] = ov.T
    return out


def _softmax_cols(parts):
    m = functools.reduce(jnp.maximum, [jnp.max(s, axis=0, keepdims=True) for s in parts])
    ps = [jnp.exp(s - m) for s in parts]
    l = functools.reduce(jnp.add, [jnp.sum(p, axis=0, keepdims=True) for p in ps])
    return ps, l


def _attn_kernel(q_ref, kv_ref, kcmp_ref, vcmp_ref, gate_ref, ovt_ref, o_ref,
                 ksel_ref, kwin_ref, vselt_ref, vwint_ref, vcmpt_ref, qsel_ref, acc_ref, *, seq):
    tq = ATT_Q
    pw = 2 * tq
    n_blk = seq // SLC_BLOCK
    w_len = WINDOW + tq
    t = pl.program_id(1)
    t0 = pl.multiple_of(t * tq, tq)
    sub = lax.broadcasted_iota(jnp.int32, (LANES, pw), 0)
    tok_off = lax.broadcasted_iota(jnp.int32, (LANES, pw), 1) & (tq - 1)
    causal = sub <= tok_off
    zero_bf = jnp.zeros((), BF16)

    @pl.when(t == 0)
    def _build_keys():
        key_blk = lax.broadcasted_iota(jnp.int32, (seq, LANES), 0) // SLC_BLOCK
        lane_s = lax.broadcasted_iota(jnp.int32, (seq, LANES), 1)
        lane_w = lax.broadcasted_iota(jnp.int32, (WINDOW, LANES), 1)
        for g in range(N_KV):
            other = (1 - g) * HEAD_DIM
            mine = (lane_s // HEAD_DIM) == g
            onehot = jnp.where(lane_s - other == key_blk, 1.0, 0.0).astype(BF16)
            ksel_ref[g] = jnp.where(mine, kv_ref[0, :, 0:KV_W], onehot)
            kwin_ref[g, WINDOW:, :] = jnp.where(mine, kv_ref[0, :, 2 * KV_W:3 * KV_W], zero_bf)
            kwin_ref[g, 0:WINDOW, :] = jnp.where(lane_w == other, 1.0, 0.0).astype(BF16)
        vselt_ref[...] = kv_ref[0, :, KV_W:2 * KV_W].astype(F32).T.astype(BF16)
        vwint_ref[:, 0:WINDOW] = jnp.zeros((LANES, WINDOW), BF16)
        vwint_ref[:, WINDOW:] = kv_ref[0, :, 3 * KV_W:4 * KV_W].astype(F32).T.astype(BF16)
        vcmpt_ref[...] = vcmp_ref[0].astype(F32).T.astype(BF16)

    gates_t = gate_ref[...].T

    def gate_row(branch, h0):
        r = branch * N_HEADS + h0
        return jnp.concatenate([gates_t[r:r + 1], gates_t[r + 1:r + 2]], axis=1)

    pairs = range(N_HEADS // 2)
    group_of = [pi // (GROUP // 2) for pi in pairs]
    other_of = [(1 - g) * HEAD_DIM for g in range(N_KV)]
    mine_rows = [(sub // HEAD_DIM) == g for g in range(N_KV)]


    qts, q_plain, q_mark = [], [], []
    for pi in pairs:
        g = group_of[pi]
        raw_t = q_ref[0, :, pi * LANES:(pi + 1) * LANES].astype(F32).T
        swap_t = jnp.concatenate([raw_t[HEAD_DIM:], raw_t[:HEAD_DIM]], axis=0)
        even_t, odd_t = (raw_t, swap_t) if g == 0 else (swap_t, raw_t)
        q_t = jnp.concatenate([even_t, odd_t], axis=1)
        qts.append(q_t)
        q_plain.append(jnp.where(mine_rows[g], q_t, 0.0).astype(BF16))
        marker = jnp.where(sub == other_of[g], NEG_INF, 0.0)
        q_mark.append(jnp.where(mine_rows[g], q_t, marker).astype(BF16))

    s_cmp = [_dot(kcmp_ref[0], q_plain[pi]) for pi in pairs]
    s_win = [_dot(kwin_ref[group_of[pi], pl.ds(t0, w_len), :], q_mark[pi]) for pi in pairs]
    s_dg = [_dot(ksel_ref[group_of[pi], pl.ds(t0, tq), :], q_plain[pi]) for pi in pairs]

    valid = sub * CMP_STRIDE + (CMP_BLOCK - 1) <= t0 + tok_off
    p_cmp = []
    for pi in pairs:
        s = jnp.where(valid, s_cmp[pi], NEG_INF)
        e = jnp.exp(s - jnp.max(s, axis=0, keepdims=True))
        p_cmp.append(jnp.where(valid, e / jnp.sum(e, axis=0, keepdims=True), 0.0))
    o_cmps = [_dot(vcmpt_ref[...], p.astype(BF16)) for p in p_cmp]
    imp_ts = []
    for g in range(N_KV):
        psum = functools.reduce(jnp.add, [p_cmp[pi][:, :tq] + p_cmp[pi][:, tq:]
                                          for pi in pairs if group_of[pi] == g])
        imp_ts.append(jnp.dot(ovt_ref[g], psum, preferred_element_type=F32,
                              precision=lax.Precision.HIGHEST))

    p_win, l_win = [], []
    for pi in pairs:
        sw = s_win[pi]
        parts = [jnp.where(sub > tok_off, sw[0:tq], NEG_INF), sw[tq:WINDOW],
                 jnp.where(causal, sw[WINDOW:], NEG_INF)]
        ps, l = _softmax_cols(parts)
        p_win.append(jnp.concatenate(ps, axis=0).astype(BF16))
        l_win.append(l)
    v_w = vwint_ref[:, pl.ds(t0, w_len)]
    o_wins = [_dot(v_w, p_win[pi]) / l_win[pi] for pi in pairs]

    blk_id = lax.broadcasted_iota(jnp.int32, (n_blk, tq), 0)
    cur = (t0 + lax.broadcasted_iota(jnp.int32, (n_blk, tq), 1)) // SLC_BLOCK
    forced = (blk_id == 0) | (blk_id == cur) | (blk_id == cur - 1)
    for g in range(N_KV):
        other = other_of[g]
        imp = imp_ts[g][other:other + n_blk, :]
        imp = jnp.where(blk_id <= cur, imp + jnp.where(forced, FORCE_BONUS, 0.0), NEG_INF)
        rank = jnp.zeros((n_blk, tq), F32)
        for j in range(n_blk):
            other_blk = imp[j:j + 1, :]
            ahead = (other_blk > imp) | ((other_blk == imp) & (blk_id > j))
            rank = rank + jnp.where(ahead, 1.0, 0.0)
        keep = (rank < SLC_TOPK) & (blk_id < t * (tq // SLC_BLOCK))
        pen = jnp.where(keep, 0.0, NEG_INF)
        pen = jnp.concatenate(
            [jnp.zeros((n, tq), F32) for n in (other,) if n] + [pen]
            + [jnp.zeros((n, tq), F32) for n in (LANES - other - n_blk,) if n], axis=0)
        pen = jnp.concatenate([pen, pen], axis=1)
        for pi in pairs:
            if group_of[pi] == g:
                qsel_ref[pi] = jnp.where(mine_rows[g], qts[pi], pen).astype(BF16)

    m0, l0, p_dg = [], [], []
    for pi in pairs:
        s_d = jnp.where(causal, s_dg[pi], NEG_INF)
        m_d = jnp.max(s_d, axis=0, keepdims=True)
        p_d = jnp.exp(s_d - m_d)
        m0.append(m_d)
        l0.append(jnp.sum(p_d, axis=0, keepdims=True))
        p_dg.append(p_d.astype(BF16))
    v_d = vselt_ref[:, pl.ds(t0, tq)]
    for pi in pairs:
        acc_ref[pi] = _dot(v_d, p_dg[pi])

    n_past = (t * tq + ATT_PAST - 1) // ATT_PAST
    last = jnp.maximum(n_past - 1, 0)

    def score_chunk(c):
        k0 = pl.multiple_of(c * ATT_PAST, ATT_PAST)
        return tuple(_dot(ksel_ref[group_of[pi], pl.ds(k0, ATT_PAST), :], qsel_ref[pi]) for pi in pairs)

    def past_body(c, carry):
        ms, ls, scores = carry
        nxt = score_chunk(jnp.minimum(c + 1, last))
        new_m, new_l, probs, alphas = [], [], [], []
        for pi in pairs:
            s = scores[pi]
            m_new = jnp.maximum(ms[pi], jnp.max(s, axis=0, keepdims=True))
            alpha = jnp.exp(ms[pi] - m_new)
            p = jnp.exp(s - m_new)
            new_l.append(alpha * ls[pi] + jnp.sum(p, axis=0, keepdims=True))
            new_m.append(m_new)
            probs.append(p.astype(BF16))
            alphas.append(alpha)
        v_c = vselt_ref[:, pl.ds(pl.multiple_of(c * ATT_PAST, ATT_PAST), ATT_PAST)]
        for pi in pairs:
            acc_ref[pi] = alphas[pi] * acc_ref[pi] + _dot(v_c, probs[pi])
        return tuple(new_m), tuple(new_l), nxt

    _, l_sel, _ = lax.fori_loop(0, n_past, past_body, (tuple(m0), tuple(l0), score_chunk(0)))

    head_out = [None] * N_HEADS
    for pi in pairs:
        g, h0 = group_of[pi], 2 * pi
        mixed = (gate_row(0, h0) * o_cmps[pi] + gate_row(1, h0) * (acc_ref[pi] / l_sel[pi])
                 + gate_row(2, h0) * o_wins[pi])
        mixed = mixed[g * HEAD_DIM:(g + 1) * HEAD_DIM]
        head_out[h0] = mixed[:, :tq]
        head_out[h0 + 1] = mixed[:, tq:]

    o_ref[0] = jnp.concatenate(head_out, axis=0).T.astype(BF16)


def _attention(q, kvsw, kcmp, vcmp, gates, batch, seq):
    n_rows = seq // CMP_STRIDE
    ovt = jnp.asarray(_overlap_matrix(n_rows, seq // SLC_BLOCK))
    tile = lambda b, t: (b, t, 0)
    per_b = lambda b, t: (b, 0, 0)
    return pl.pallas_call(
        functools.partial(_attn_kernel, seq=seq),
        grid=(batch, seq // ATT_Q),
        in_specs=[pl.BlockSpec((1, ATT_Q, NSA_W), tile),
                  pl.BlockSpec((1, seq, 4 * KV_W), per_b),
                  pl.BlockSpec((1, n_rows, KV_W), per_b),
                  pl.BlockSpec((1, n_rows, KV_W), per_b),
                  pl.BlockSpec((ATT_Q, LANES), lambda b, t: (b * (seq // ATT_Q) + t, 0)),
                  pl.BlockSpec((N_KV, LANES, n_rows), lambda b, t: (0, 0, 0))],
        out_specs=pl.BlockSpec((1, ATT_Q, NSA_W), tile),
        out_shape=jax.ShapeDtypeStruct((batch, seq, NSA_W), BF16),
        scratch_shapes=[pltpu.VMEM((N_KV, seq, LANES), BF16),
                        pltpu.VMEM((N_KV, WINDOW + seq, LANES), BF16),
                        pltpu.VMEM((LANES, seq), BF16),
                        pltpu.VMEM((LANES, WINDOW + seq), BF16),
                        pltpu.VMEM((LANES, n_rows), BF16),
                        pltpu.VMEM((N_HEADS // 2, LANES, 2 * ATT_Q), BF16),
                        pltpu.VMEM((N_HEADS // 2, LANES, 2 * ATT_Q), F32)],
        compiler_params=_cparams(("arbitrary", "arbitrary")),
        name="nsa_attention",
    )(q.reshape(batch, seq, NSA_W), kvsw.reshape(batch, seq, 4 * KV_W), kcmp, vcmp, gates, ovt)


def _even_out_kernel(a_ref, sa_ref, usb_ref, vn_ref, ws_ref, bias_ref, w_ref, g_ref, x_ref, o_ref):
    lane = lax.broadcasted_iota(jnp.int32, (SGU_CHUNK, LANES), 1)
    low = lane < SGU_GROUP_DIM
    ri = lax.broadcasted_iota(jnp.int32, (SGU_CHUNK, SGU_CHUNK), 0)
    ci = lax.broadcasted_iota(jnp.int32, (SGU_CHUNK, SGU_CHUNK), 1)
    tril = ci <= ri
    wmix = [jnp.where(tril, ws_ref[g], 0.0).astype(BF16) for g in range(SGU_GROUPS)]
    zero = jnp.zeros((), BF16)
    left = (a_ref[...].astype(F32) * sa_ref[...].astype(F32)).astype(BF16)
    rights = []
    for c in range(OUT_ROWS // SGU_CHUNK):
        rows = slice(c * SGU_CHUNK, (c + 1) * SGU_CHUNK)
        blocks = []
        for p in range(SGU_W // LANES):
            cols = slice(p * LANES, (p + 1) * LANES)
            vb = vn_ref[rows, cols]
            mixed = (_dot(wmix[2 * p], jnp.where(low, vb, zero))
                     + _dot(wmix[2 * p + 1], jnp.where(low, zero, vb))
                     + bias_ref[:, cols])
            blocks.append((usb_ref[rows, cols].astype(F32) * mixed).astype(BF16))
        rights.append(jnp.concatenate(blocks, axis=1))
    right = jnp.concatenate(rights, axis=0)
    y = _dot(left, w_ref[0:NSA_W, :]) + _dot(right, w_ref[NSA_W:, :])
    o_ref[...] = x_ref[...] + _rms(y, g_ref[...])


def _even_out(a, sa, usb, vn, sgu_w, sgu_b, w_out, g_post, x2):
    t, d = x2.shape
    bias = jnp.repeat(sgu_b.T, SGU_GROUP_DIM, axis=1)
    row = lambda i: (i, 0)
    fixed = lambda i: (0, 0)
    return pl.pallas_call(
        _even_out_kernel,
        grid=(t // OUT_ROWS,),
        in_specs=[pl.BlockSpec((OUT_ROWS, NSA_W), row), pl.BlockSpec((OUT_ROWS, NSA_W), row),
                  pl.BlockSpec((OUT_ROWS, SGU_W), row), pl.BlockSpec((OUT_ROWS, SGU_W), row),
                  pl.BlockSpec((SGU_GROUPS, SGU_CHUNK, SGU_CHUNK), lambda i: (0, 0, 0)),
                  pl.BlockSpec((SGU_CHUNK, SGU_W), fixed),
                  pl.BlockSpec((NSA_W + SGU_W, d), fixed), pl.BlockSpec((1, d), fixed),
                  pl.BlockSpec((OUT_ROWS, d), row)],
        out_specs=pl.BlockSpec((OUT_ROWS, d), row),
        out_shape=jax.ShapeDtypeStruct((t, d), F32),
        compiler_params=_cparams(("arbitrary",)),
        name="even_out",
    )(a, sa, usb, vn, sgu_w, bias, w_out.astype(BF16), g_post.reshape(1, d), x2)


def _odd_kernel(x_ref, gpre_ref, win_ref, dww_ref, dwb_ref, lng_ref, lnb_ref, wout_ref, gpost_ref,
                o_ref, ypad_ref, conv_ref):
    d = x_ref.shape[-1]
    x = x_ref[0]
    h = _rms(x, gpre_ref[...]).astype(BF16)
    a = _dot(h, win_ref[:, 0:d])
    gl = _dot(h, win_ref[:, d:2 * d])

    halo_blocks = CONV_HALO // SUBLANES
    tile_blocks = ODD_ROWS // SUBLANES

    @pl.when(pl.program_id(1) == 0)
    def _():
        ypad_ref[0:halo_blocks] = jnp.zeros((halo_blocks, SUBLANES, d), F32)

    ypad_ref[halo_blocks:] = (a * jax.nn.sigmoid(gl)).reshape(tile_blocks, SUBLANES, d)

    shift = CONV_HALO - (CONV_K - 1)
    blocks_per_step = CONV_RB // SUBLANES
    for lt in range(d // LANES):
        cols = slice(lt * LANES, (lt + 1) * LANES)
        wt = [dww_ref[j, :, cols] for j in range(CONV_SPAN)]
        bias = jnp.broadcast_to(dwb_ref[:, cols], (SUBLANES, LANES))

        def conv_body(i, carry, cols=cols, wt=wt, bias=bias):
            parts = [[None] * CONV_CHAINS for _ in range(blocks_per_step)]
            for j in range(CONV_SPAN + CONV_RB - SUBLANES):
                src = shift + j
                row = ypad_ref[i * blocks_per_step + src // SUBLANES,
                               src % SUBLANES:src % SUBLANES + 1, cols]
                row = jnp.broadcast_to(row, (SUBLANES, LANES))
                for blk in range(blocks_per_step):
                    jj = j - blk * SUBLANES
                    if 0 <= jj < CONV_SPAN:
                        term = wt[jj] * row
                        prev = parts[blk][jj % CONV_CHAINS]
                        parts[blk][jj % CONV_CHAINS] = term if prev is None else prev + term
            for blk in range(blocks_per_step):
                r0 = pl.multiple_of((i * blocks_per_step + blk) * SUBLANES, SUBLANES)
                conv_ref[pl.ds(r0, SUBLANES), cols] = functools.reduce(jnp.add, parts[blk]) + bias
            return carry

        lax.fori_loop(0, ODD_ROWS // CONV_RB, conv_body, 0)
    ypad_ref[0:halo_blocks] = ypad_ref[tile_blocks:tile_blocks + halo_blocks]

    y = jax.nn.silu(_layer_norm(conv_ref[...], lng_ref[...], lnb_ref[...]))
    z = _dot(h, win_ref[:, 2 * d:3 * d])
    out = _dot((y * jax.nn.silu(z)).astype(BF16), wout_ref[...])
    o_ref[0] = x + _rms(out, gpost_ref[...])


def _odd_layer(x3, g_pre, w_in, dw_w, dw_b, ln_g, ln_b, w_out, g_post):
    batch, seq, d = x3.shape
    tile = lambda b, t: (b, t, 0)
    fixed = lambda b, t: (0, 0)
    vec = lambda v: v.reshape(1, d)
    tap = np.arange(CONV_SPAN)[:, None] - np.arange(SUBLANES)[None, :]
    dww = jnp.where(((tap >= 0) & (tap < CONV_K))[:, :, None],
                    dw_w[np.clip(tap, 0, CONV_K - 1)], 0.0)
    return pl.pallas_call(
        _odd_kernel,
        grid=(batch, seq // ODD_ROWS),
        in_specs=[pl.BlockSpec((1, ODD_ROWS, d), tile), pl.BlockSpec((1, d), fixed),
                  pl.BlockSpec((d, 3 * d), fixed),
                  pl.BlockSpec((CONV_SPAN, SUBLANES, d), lambda b, t: (0, 0, 0)),
                  pl.BlockSpec((1, d), fixed), pl.BlockSpec((1, d), fixed), pl.BlockSpec((1, d), fixed),
                  pl.BlockSpec((d, d), fixed), pl.BlockSpec((1, d), fixed)],
        out_specs=pl.BlockSpec((1, ODD_ROWS, d), tile),
        out_shape=jax.ShapeDtypeStruct((batch, seq, d), F32),
        scratch_shapes=[pltpu.VMEM(((CONV_HALO + ODD_ROWS) // SUBLANES, SUBLANES, d), F32),
                        pltpu.VMEM((ODD_ROWS, d), F32)],
        compiler_params=_cparams(("arbitrary", "arbitrary")),
        name="odd_layer",
    )(x3, vec(g_pre), w_in.astype(BF16), dww, vec(dw_b), vec(ln_g), vec(ln_b),
      w_out.astype(BF16), vec(g_post))


def kernel(x, norm_pre, norm_post, even_w_in, even_cmp_k_pe, even_cmp_k_w1, even_cmp_k_w2,
           even_cmp_v_pe, even_cmp_v_w1, even_cmp_v_w2, even_sgu_ln_g, even_sgu_ln_b,
           even_sgu_w, even_sgu_b, even_w_out, odd_w_in, odd_dw_w, odd_dw_b, odd_ln_g,
           odd_ln_b, odd_w_out):
    batch, seq, d = x.shape
    depth = norm_pre.shape[0]
    for i in range(depth):
        li = i // 2
        if i % 2 == 0:
            x2 = x.reshape(batch * seq, d)
            q, kcvc, kvsw, sa, usb, vn, gates = _even_in(
                x2, norm_pre[i], even_w_in[li], even_sgu_ln_g[li], even_sgu_ln_b[li])
            cw = _compress_weights(even_cmp_k_pe[li], even_cmp_k_w1[li], even_cmp_k_w2[li],
                                   even_cmp_v_pe[li], even_cmp_v_w1[li], even_cmp_v_w2[li])
            kcmp, vcmp = _compress(kcvc, batch, seq, cw)
            a = _attention(q, kvsw, kcmp, vcmp, gates, batch, seq)
            x = _even_out(a.reshape(batch * seq, NSA_W), sa, usb, vn, even_sgu_w[li], even_sgu_b[li],
                          even_w_out[li], norm_post[i], x2).reshape(batch, seq, d)
        else:
            x = _odd_layer(x, norm_pre[i], odd_w_in[li], odd_dw_w[li], odd_dw_b[li],
                           odd_ln_g[li], odd_ln_b[li], odd_w_out[li], norm_post[i])
    return x
```

```python
import functools

import jax
import jax.numpy as jnp
import numpy as np
from jax import lax
from jax.experimental import pallas as pl
from jax.experimental.pallas import tpu as pltpu

N_HEADS = 8
N_KV = 2
GROUP = N_HEADS // N_KV
HEAD_DIM = 64
NSA_W = N_HEADS * HEAD_DIM
KV_W = N_KV * HEAD_DIM
CMP_BLOCK = 32
CMP_STRIDE = 16
SLC_BLOCK = 64
SLC_TOPK = 8
WINDOW = 512
FORCE_BONUS = 1e4
NEG_INF = -1e30
TAKEN = -3e38
LOG2E = 1.4426950408889634
SGU_GROUPS = 8
SGU_GROUP_DIM = 64
SGU_W = SGU_GROUPS * SGU_GROUP_DIM
SGU_CHUNK = 128
CONV_K = 31
RMS_EPS = 1e-6
LN_EPS = 1e-5

LANES = 128
SUBLANES = 8
VMEM_LIMIT_BYTES = 56 * 1024 * 1024

IN_ROWS = 512
ATT_Q = 128
ATT_PAST = 256
OUT_ROWS = 512
ODD_ROWS = 512
CONV_HALO = 32
CONV_RB = 32
CONV_CHAINS = 2
CONV_SPAN = CONV_K + SUBLANES - 1

F32 = jnp.float32
BF16 = jnp.bfloat16


def _cparams(sem):
    return pltpu.CompilerParams(dimension_semantics=sem, vmem_limit_bytes=VMEM_LIMIT_BYTES)


def _rms(x, g):
    return x * lax.rsqrt(jnp.mean(x * x, axis=-1, keepdims=True) + RMS_EPS) * g


def _layer_norm(x, g, b):
    mu = jnp.mean(x, axis=-1, keepdims=True)
    xc = x - mu
    return xc * lax.rsqrt(jnp.mean(xc * xc, axis=-1, keepdims=True) + LN_EPS) * g + b


def _dot(a, b):
    return jnp.dot(a, b, preferred_element_type=F32)


EV_Q, EV_KC, EV_KS, EV_ZA, EV_U, EV_V, EV_ZB, EV_G, EV_END = 0, 512, 768, 1280, 1792, 2304, 2816, 3328, 3456


def _even_in_kernel(x_ref, g_ref, w_ref, lng_ref, lnb_ref,
                    q_ref, kcvc_ref, kvsw_ref, sa_ref, usb_ref, vn_ref, gate_ref):
    h = _rms(x_ref[...], g_ref[...]).astype(BF16)
    q_ref[...] = (_dot(h, w_ref[:, EV_Q:EV_KC]) * (HEAD_DIM ** -0.5 * LOG2E)).astype(BF16)
    kcvc_ref[...] = _dot(h, w_ref[:, EV_KC:EV_KS]).astype(BF16)
    kvsw_ref[...] = _dot(h, w_ref[:, EV_KS:EV_ZA]).astype(BF16)
    sa_ref[...] = jax.nn.silu(_dot(h, w_ref[:, EV_ZA:EV_U])).astype(BF16)
    u = _dot(h, w_ref[:, EV_U:EV_V])
    zb = _dot(h, w_ref[:, EV_ZB:EV_G])
    usb_ref[...] = (jax.nn.gelu(u) * jax.nn.silu(zb)).astype(BF16)
    v = jax.nn.gelu(_dot(h, w_ref[:, EV_V:EV_ZB]))
    vn_ref[...] = _layer_norm(v, lng_ref[...], lnb_ref[...]).astype(BF16)
    gate_ref[...] = jax.nn.sigmoid(_dot(h, w_ref[:, EV_G:EV_END]))


def _even_in(x2, g_pre, w_in, ln_g, ln_b):
    t, d = x2.shape
    w = jnp.concatenate(
        [w_in[:, :1280], w_in[:, 1304:], w_in[:, 1280:1304],
         jnp.zeros((d, LANES - 3 * N_HEADS), w_in.dtype)], axis=1).astype(BF16)
    row = lambda i: (i, 0)
    fixed = lambda i: (0, 0)
    outs = [(NSA_W, BF16), (2 * KV_W, BF16), (4 * KV_W, BF16), (NSA_W, BF16),
            (SGU_W, BF16), (SGU_W, BF16), (LANES, F32)]
    return pl.pallas_call(
        _even_in_kernel,
        grid=(t // IN_ROWS,),
        in_specs=[pl.BlockSpec((IN_ROWS, d), row), pl.BlockSpec((1, d), fixed),
                  pl.BlockSpec((d, EV_END), fixed), pl.BlockSpec((1, SGU_W), fixed),
                  pl.BlockSpec((1, SGU_W), fixed)],
        out_specs=[pl.BlockSpec((IN_ROWS, n), row) for n, _ in outs],
        out_shape=[jax.ShapeDtypeStruct((t, n), dt) for n, dt in outs],
        compiler_params=_cparams(("arbitrary",)),
        name="even_in",
    )(x2, g_pre.reshape(1, d), w, ln_g.reshape(1, SGU_W), ln_b.reshape(1, SGU_W))


def _compress_kernel(kr_ref, pea_ref, peb_ref, w1a_ref, w1b_ref, w2_ref, kcmp_ref, vcmp_ref):
    kr = kr_ref[0].astype(F32)
    n_rows = kr.shape[0]
    first = _dot((kr + pea_ref[...]).astype(BF16), w1a_ref[...])
    second = _dot((kr + peb_ref[...]).astype(BF16), w1b_ref[...])
    pre = first + pltpu.roll(second, n_rows - 1, 0)
    out = _dot(jax.nn.silu(pre).astype(BF16), w2_ref[...])
    rows = lax.broadcasted_iota(jnp.int32, out.shape, 0)
    out = jnp.where(rows < n_rows - 1, out, 0.0)
    kcmp_ref[0] = out[:, :KV_W].astype(BF16)
    vcmp_ref[0] = out[:, KV_W:].astype(BF16)


def _compress_weights(k_pe, k_w1, k_w2, v_pe, v_w1, v_w2):
    dh, half = HEAD_DIM, CMP_STRIDE
    eye_g = jnp.eye(N_KV, dtype=F32)

    def first_layer(w1, lo):
        w = w1.reshape(CMP_BLOCK, dh, dh)[lo:lo + half]
        return jnp.einsum('lde,gh->lgdhe', w, eye_g).reshape(half, KV_W, KV_W)

    def both(lo):
        wk, wv = first_layer(k_w1, lo), first_layer(v_w1, lo)
        z = jnp.zeros_like(wk)
        top = jnp.concatenate([wk, z], axis=2)
        bot = jnp.concatenate([z, wv], axis=2)
        return jnp.concatenate([top, bot], axis=1).reshape(half * 2 * KV_W, 2 * KV_W).astype(BF16)

    def pe_row(lo):
        pk = jnp.tile(k_pe[lo:lo + half], (1, N_KV))
        pv = jnp.tile(v_pe[lo:lo + half], (1, N_KV))
        return jnp.concatenate([pk, pv], axis=1).reshape(1, half * 2 * KV_W)

    w2 = jnp.zeros((2 * KV_W, 2 * KV_W), F32)
    for i, w in enumerate((k_w2, k_w2, v_w2, v_w2)):
        w2 = w2.at[i * dh:(i + 1) * dh, i * dh:(i + 1) * dh].set(w)
    return pe_row(0), pe_row(half), both(0), both(half), w2.astype(BF16)


def _compress(kcvc, batch, seq, cw):
    pea, peb, w1a, w1b, w2 = cw
    n_rows = seq // CMP_STRIDE
    width = CMP_STRIDE * 2 * KV_W
    kr = kcvc.reshape(batch, n_rows, width)
    fixed = lambda b: (0, 0)
    per_b = lambda b: (b, 0, 0)
    return pl.pallas_call(
        _compress_kernel,
        grid=(batch,),
        in_specs=[pl.BlockSpec((1, n_rows, width), per_b),
                  pl.BlockSpec((1, width), fixed), pl.BlockSpec((1, width), fixed),
                  pl.BlockSpec((width, 2 * KV_W), fixed), pl.BlockSpec((width, 2 * KV_W), fixed),
                  pl.BlockSpec((2 * KV_W, 2 * KV_W), fixed)],
        out_specs=[pl.BlockSpec((1, n_rows, KV_W), per_b)] * 2,
        out_shape=[jax.ShapeDtypeStruct((batch, n_rows, KV_W), BF16)] * 2,
        compiler_params=_cparams(("arbitrary",)),
        name="compress",
    )(kr, pea, peb, w1a, w1b, w2)


def _overlap_matrix(n_rows, n_blk):
    n_cmp = n_rows - 1
    tok = np.arange(n_cmp)[:, None] * CMP_STRIDE + np.arange(CMP_BLOCK)[None, :]
    ov = ((tok[:, :, None] // SLC_BLOCK) == np.arange(n_blk)[None, None, :]).sum(1)
    ov = ov.astype(np.float32) / np.float32(CMP_BLOCK)
    out = np.zeros((N_KV, LANES, n_rows), np.float32)
    for g in range(N_KV):
        other = (1 - g) * HEAD_DIM
        out[g, other:other + n_blk, :n_cmp] = ov.T
    return out


def _attn_kernel(q_ref, kv_ref, kcmp_ref, vcmp_ref, gate_ref, ovt_ref, o_ref,
                 ksel_ref, kwin_ref, vselt_ref, vwint_ref, vcmpt_ref, qsel_ref, acc_ref,
                 sc_ref, pr_ref, *, seq):
    tq = ATT_Q
    pw = 2 * tq
    kc = ATT_PAST
    n_blk = seq // SLC_BLOCK
    w_len = WINDOW + tq
    t = pl.program_id(1)
    t0 = pl.multiple_of(t * tq, tq)
    sub = lax.broadcasted_iota(jnp.int32, (LANES, pw), 0)
    tok_off = lax.broadcasted_iota(jnp.int32, (LANES, pw), 1) & (tq - 1)
    causal = sub <= tok_off
    zero_bf = jnp.zeros((), BF16)

    @pl.when(t == 0)
    def _build_keys():
        key_blk = lax.broadcasted_iota(jnp.int32, (seq, LANES), 0) // SLC_BLOCK
        lane_s = lax.broadcasted_iota(jnp.int32, (seq, LANES), 1)
        lane_w = lax.broadcasted_iota(jnp.int32, (WINDOW, LANES), 1)
        vs_t = kv_ref[0, :, KV_W:2 * KV_W].astype(F32).T
        vw_t = kv_ref[0, :, 3 * KV_W:4 * KV_W].astype(F32).T
        vc_t = vcmp_ref[0].astype(F32).T
        for g in range(N_KV):
            other = (1 - g) * HEAD_DIM
            mine = (lane_s // HEAD_DIM) == g
            onehot = jnp.where(lane_s - other == key_blk, 1.0, 0.0).astype(BF16)
            ksel_ref[g] = jnp.where(mine, kv_ref[0, :, 0:KV_W], onehot)
            kwin_ref[g, WINDOW:, :] = jnp.where(mine, kv_ref[0, :, 2 * KV_W:3 * KV_W], zero_bf)
            kwin_ref[g, 0:WINDOW, :] = jnp.where(lane_w == other, 1.0, 0.0).astype(BF16)

            def with_ones(v_t):
                row = lax.broadcasted_iota(jnp.int32, v_t.shape, 0)
                return jnp.where(row // HEAD_DIM == g, v_t, jnp.where(row == other, 1.0, 0.0)).astype(BF16)

            vselt_ref[g] = with_ones(vs_t)
            vwint_ref[g, :, 0:WINDOW] = with_ones(jnp.zeros((LANES, WINDOW), F32))
            vwint_ref[g, :, WINDOW:] = with_ones(vw_t)
            vcmpt_ref[g] = with_ones(vc_t)

    gates_t = gate_ref[...].T

    def gate_row(branch, h0):
        r = branch * N_HEADS + h0
        return jnp.concatenate([gates_t[r:r + 1], gates_t[r + 1:r + 2]], axis=1)

    pairs = range(N_HEADS // 2)
    group_of = [pi // (GROUP // 2) for pi in pairs]
    other_of = [(1 - g) * HEAD_DIM for g in range(N_KV)]
    mine_rows = [(sub // HEAD_DIM) == g for g in range(N_KV)]

    def normalised(o_t, pi):
        g = group_of[pi]
        return o_t[g * HEAD_DIM:(g + 1) * HEAD_DIM] / o_t[other_of[g]:other_of[g] + 1]


    qts, q_plain, q_mark = [], [], []
    for pi in pairs:
        g = group_of[pi]
        raw_t = q_ref[0, :, pi * LANES:(pi + 1) * LANES].astype(F32).T
        swap_t = jnp.concatenate([raw_t[HEAD_DIM:], raw_t[:HEAD_DIM]], axis=0)
        even_t, odd_t = (raw_t, swap_t) if g == 0 else (swap_t, raw_t)
        q_t = jnp.concatenate([even_t, odd_t], axis=1)
        qts.append(q_t)
        q_plain.append(jnp.where(mine_rows[g], q_t, 0.0).astype(BF16))
        marker = jnp.where(sub == other_of[g], NEG_INF, 0.0)
        q_mark.append(jnp.where(mine_rows[g], q_t, marker).astype(BF16))

    s_cmp = [_dot(kcmp_ref[0], q_plain[pi]) for pi in pairs]
    s_win = [_dot(kwin_ref[group_of[pi], pl.ds(t0, w_len), :], q_mark[pi]) for pi in pairs]
    s_dg = [_dot(ksel_ref[group_of[pi], pl.ds(t0, tq), :], q_plain[pi]) for pi in pairs]

    valid = sub * CMP_STRIDE + (CMP_BLOCK - 1) <= t0 + tok_off
    p_cmp = []
    for pi in pairs:
        s = jnp.where(valid, s_cmp[pi], NEG_INF)
        e = jnp.exp2(s - jnp.max(s, axis=0, keepdims=True))
        p_cmp.append(jnp.where(valid, e / jnp.sum(e, axis=0, keepdims=True), 0.0))
    o_cmps = [_dot(vcmpt_ref[group_of[pi]], p_cmp[pi].astype(BF16))[group_of[pi] * HEAD_DIM:
                                                                      (group_of[pi] + 1) * HEAD_DIM]
              for pi in pairs]
    imp_ts = []
    for g in range(N_KV):
        psum = functools.reduce(jnp.add, [p_cmp[pi][:, :tq] + p_cmp[pi][:, tq:]
                                          for pi in pairs if group_of[pi] == g])
        imp_ts.append(jnp.dot(ovt_ref[g], psum, preferred_element_type=F32,
                              precision=lax.Precision.HIGHEST))

    p_win = []
    for pi in pairs:
        sw = s_win[pi]
        parts = [jnp.where(sub > tok_off, sw[0:tq], NEG_INF), sw[tq:WINDOW],
                 jnp.where(causal, sw[WINDOW:], NEG_INF)]
        m = functools.reduce(jnp.maximum, [jnp.max(s, axis=0, keepdims=True) for s in parts])
        p_win.append(jnp.concatenate([jnp.exp2(s - m).astype(BF16) for s in parts], axis=0))
    o_wins = [normalised(_dot(vwint_ref[group_of[pi], :, pl.ds(t0, w_len)], p_win[pi]), pi) for pi in pairs]

    blk_id = lax.broadcasted_iota(jnp.int32, (n_blk, tq), 0)
    blk_f = blk_id.astype(F32)
    cur = (t0 + lax.broadcasted_iota(jnp.int32, (n_blk, tq), 1)) // SLC_BLOCK
    forced = (blk_id == 0) | (blk_id == cur) | (blk_id == cur - 1)
    for g in range(N_KV):
        other = other_of[g]
        imp = imp_ts[g][other:other + n_blk, :]
        imp = jnp.where(blk_id <= cur, imp + jnp.where(forced, FORCE_BONUS, 0.0), NEG_INF)
        chosen = jnp.zeros((n_blk, tq), F32)
        for _ in range(min(SLC_TOPK, n_blk)):
            top = jnp.max(imp, axis=0, keepdims=True)
            first = jnp.min(jnp.where(imp == top, blk_f, float(n_blk)), axis=0, keepdims=True)
            hit = blk_f == first
            chosen = jnp.where(hit, 1.0, chosen)
            imp = jnp.where(hit, TAKEN, imp)
        keep = (chosen > 0.5) & (blk_id < t * (tq // SLC_BLOCK))
        pen = jnp.where(keep, 0.0, NEG_INF)
        pen = jnp.concatenate(
            [jnp.zeros((n, tq), F32) for n in (other,) if n] + [pen]
            + [jnp.zeros((n, tq), F32) for n in (LANES - other - n_blk,) if n], axis=0)
        pen = jnp.concatenate([pen, pen], axis=1)
        for pi in pairs:
            if group_of[pi] == g:
                qsel_ref[pi] = jnp.where(mine_rows[g], qts[pi], pen).astype(BF16)

    m0, p_dg = [], []
    for pi in pairs:
        s_d = jnp.where(causal, s_dg[pi], NEG_INF)
        m_d = jnp.max(s_d, axis=0, keepdims=True)
        m0.append(m_d)
        p_dg.append(jnp.exp2(s_d - m_d).astype(BF16))
    for pi in pairs:
        acc_ref[pi] = _dot(vselt_ref[group_of[pi], :, pl.ds(t0, tq)], p_dg[pi])

    n_past = (t * tq + kc - 1) // kc
    max_chunk = seq // kc - 1

    def chunk_start(c):
        return pl.multiple_of(jnp.clip(c, 0, max_chunk) * kc, kc)

    def scores_into(c, slot):
        k0 = chunk_start(c)
        for pi in pairs:
            sc_ref[slot, pi] = _dot(ksel_ref[group_of[pi], pl.ds(k0, kc), :], qsel_ref[pi])

    def values_from(c, slot, alphas):
        k0 = chunk_start(c)
        for pi in pairs:
            acc_ref[pi] = alphas[pi] * acc_ref[pi] + _dot(vselt_ref[group_of[pi], :, pl.ds(k0, kc)],
                                                          pr_ref[slot, pi])

    def softmax_slot(slot, ms):
        new_m, alphas = [], []
        for pi in pairs:
            s = sc_ref[slot, pi]
            m_new = jnp.maximum(ms[pi], jnp.max(s, axis=0, keepdims=True))
            alphas.append(jnp.exp2(ms[pi] - m_new))
            pr_ref[slot, pi] = jnp.exp2(s - m_new).astype(BF16)
            new_m.append(m_new)
        return tuple(new_m), tuple(alphas)

    def past_body(i, carry):
        ms, alphas = carry
        c0 = 2 * i
        scores_into(c0 + 1, 1)
        values_from(c0 - 1, 1, alphas)
        ms, alphas = softmax_slot(0, ms)
        scores_into(c0 + 2, 0)
        values_from(c0, 0, alphas)
        return softmax_slot(1, ms)

    scores_into(0, 0)
    for pi in pairs:
        pr_ref[1, pi] = jnp.zeros((kc, pw), BF16)
    n_trips = (n_past + 1) // 2
    _, alphas = lax.fori_loop(0, n_trips, past_body,
                              (tuple(m0), tuple(jnp.ones((1, pw), F32) for _ in pairs)))
    values_from(2 * n_trips - 1, 1, alphas)

    head_out = [None] * N_HEADS
    for pi in pairs:
        h0 = 2 * pi
        mixed = (gate_row(0, h0) * o_cmps[pi] + gate_row(1, h0) * normalised(acc_ref[pi], pi)
                 + gate_row(2, h0) * o_wins[pi])
        head_out[h0] = mixed[:, :tq]
        head_out[h0 + 1] = mixed[:, tq:]

    o_ref[0] = jnp.concatenate(head_out, axis=0).T.astype(BF16)


def _attention(q, kvsw, kcmp, vcmp, gates, batch, seq):
    n_rows = seq // CMP_STRIDE
    n_pairs = N_HEADS // 2
    ovt = jnp.asarray(_overlap_matrix(n_rows, seq // SLC_BLOCK))
    tile = lambda b, t: (b, t, 0)
    per_b = lambda b, t: (b, 0, 0)
    return pl.pallas_call(
        functools.partial(_attn_kernel, seq=seq),
        grid=(batch, seq // ATT_Q),
        in_specs=[pl.BlockSpec((1, ATT_Q, NSA_W), tile),
                  pl.BlockSpec((1, seq, 4 * KV_W), per_b),
                  pl.BlockSpec((1, n_rows, KV_W), per_b),
                  pl.BlockSpec((1, n_rows, KV_W), per_b),
                  pl.BlockSpec((ATT_Q, LANES), lambda b, t: (b * (seq // ATT_Q) + t, 0)),
                  pl.BlockSpec((N_KV, LANES, n_rows), lambda b, t: (0, 0, 0))],
        out_specs=pl.BlockSpec((1, ATT_Q, NSA_W), tile),
        out_shape=jax.ShapeDtypeStruct((batch, seq, NSA_W), BF16),
        scratch_shapes=[pltpu.VMEM((N_KV, seq, LANES), BF16),
                        pltpu.VMEM((N_KV, WINDOW + seq, LANES), BF16),
                        pltpu.VMEM((N_KV, LANES, seq), BF16),
                        pltpu.VMEM((N_KV, LANES, WINDOW + seq), BF16),
                        pltpu.VMEM((N_KV, LANES, n_rows), BF16),
                        pltpu.VMEM((n_pairs, LANES, 2 * ATT_Q), BF16),
                        pltpu.VMEM((n_pairs, LANES, 2 * ATT_Q), F32),
                        pltpu.VMEM((2, n_pairs, ATT_PAST, 2 * ATT_Q), F32),
                        pltpu.VMEM((2, n_pairs, ATT_PAST, 2 * ATT_Q), BF16)],
        compiler_params=_cparams(("arbitrary", "arbitrary")),
        name="nsa_attention",
    )(q.reshape(batch, seq, NSA_W), kvsw.reshape(batch, seq, 4 * KV_W), kcmp, vcmp, gates, ovt)


def _even_out_kernel(a_ref, sa_ref, usb_ref, vn_ref, ws_ref, bias_ref, w_ref, g_ref, x_ref, o_ref):
    lane = lax.broadcasted_iota(jnp.int32, (SGU_CHUNK, LANES), 1)
    low = lane < SGU_GROUP_DIM
    ri = lax.broadcasted_iota(jnp.int32, (SGU_CHUNK, SGU_CHUNK), 0)
    ci = lax.broadcasted_iota(jnp.int32, (SGU_CHUNK, SGU_CHUNK), 1)
    tril = ci <= ri
    wmix = [jnp.where(tril, ws_ref[g], 0.0).astype(BF16) for g in range(SGU_GROUPS)]
    zero = jnp.zeros((), BF16)
    left = (a_ref[...].astype(F32) * sa_ref[...].astype(F32)).astype(BF16)
    rights = []
    for c in range(OUT_ROWS // SGU_CHUNK):
        rows = slice(c * SGU_CHUNK, (c + 1) * SGU_CHUNK)
        blocks = []
        for p in range(SGU_W // LANES):
            cols = slice(p * LANES, (p + 1) * LANES)
            vb = vn_ref[rows, cols]
            mixed = (_dot(wmix[2 * p], jnp.where(low, vb, zero))
                     + _dot(wmix[2 * p + 1], jnp.where(low, zero, vb))
                     + bias_ref[:, cols])
            blocks.append((usb_ref[rows, cols].astype(F32) * mixed).astype(BF16))
        rights.append(jnp.concatenate(blocks, axis=1))
    right = jnp.concatenate(rights, axis=0)
    y = _dot(left, w_ref[0:NSA_W, :]) + _dot(right, w_ref[NSA_W:, :])
    o_ref[...] = x_ref[...] + _rms(y, g_ref[...])


def _even_out(a, sa, usb, vn, sgu_w, sgu_b, w_out, g_post, x2):
    t, d = x2.shape
    bias = jnp.repeat(sgu_b.T, SGU_GROUP_DIM, axis=1)
    row = lambda i: (i, 0)
    fixed = lambda i: (0, 0)
    return pl.pallas_call(
        _even_out_kernel,
        grid=(t // OUT_ROWS,),
        in_specs=[pl.BlockSpec((OUT_ROWS, NSA_W), row), pl.BlockSpec((OUT_ROWS, NSA_W), row),
                  pl.BlockSpec((OUT_ROWS, SGU_W), row), pl.BlockSpec((OUT_ROWS, SGU_W), row),
                  pl.BlockSpec((SGU_GROUPS, SGU_CHUNK, SGU_CHUNK), lambda i: (0, 0, 0)),
                  pl.BlockSpec((SGU_CHUNK, SGU_W), fixed),
                  pl.BlockSpec((NSA_W + SGU_W, d), fixed), pl.BlockSpec((1, d), fixed),
                  pl.BlockSpec((OUT_ROWS, d), row)],
        out_specs=pl.BlockSpec((OUT_ROWS, d), row),
        out_shape=jax.ShapeDtypeStruct((t, d), F32),
        compiler_params=_cparams(("arbitrary",)),
        name="even_out",
    )(a, sa, usb, vn, sgu_w, bias, w_out.astype(BF16), g_post.reshape(1, d), x2)


def _odd_kernel(x_ref, gpre_ref, win_ref, dww_ref, dwb_ref, lng_ref, lnb_ref, wout_ref, gpost_ref,
                o_ref, ypad_ref, conv_ref):
    d = x_ref.shape[-1]
    x = x_ref[0]
    h = _rms(x, gpre_ref[...]).astype(BF16)
    a = _dot(h, win_ref[:, 0:d])
    gl = _dot(h, win_ref[:, d:2 * d])

    halo_blocks = CONV_HALO // SUBLANES
    tile_blocks = ODD_ROWS // SUBLANES

    @pl.when(pl.program_id(1) == 0)
    def _():
        ypad_ref[0:halo_blocks] = jnp.zeros((halo_blocks, SUBLANES, d), F32)

    ypad_ref[halo_blocks:] = (a * jax.nn.sigmoid(gl)).reshape(tile_blocks, SUBLANES, d)

    shift = CONV_HALO - (CONV_K - 1)
    blocks_per_step = CONV_RB // SUBLANES
    for lt in range(d // LANES):
        cols = slice(lt * LANES, (lt + 1) * LANES)
        wt = [dww_ref[j, :, cols] for j in range(CONV_SPAN)]
        bias = jnp.broadcast_to(dwb_ref[:, cols], (SUBLANES, LANES))

        def conv_body(i, carry, cols=cols, wt=wt, bias=bias):
            parts = [[None] * CONV_CHAINS for _ in range(blocks_per_step)]
            for j in range(CONV_SPAN + CONV_RB - SUBLANES):
                src = shift + j
                row = ypad_ref[i * blocks_per_step + src // SUBLANES,
                               src % SUBLANES:src % SUBLANES + 1, cols]
                row = jnp.broadcast_to(row, (SUBLANES, LANES))
                for blk in range(blocks_per_step):
                    jj = j - blk * SUBLANES
                    if 0 <= jj < CONV_SPAN:
                        term = wt[jj] * row
                        prev = parts[blk][jj % CONV_CHAINS]
                        parts[blk][jj % CONV_CHAINS] = term if prev is None else prev + term
            for blk in range(blocks_per_step):
                r0 = pl.multiple_of((i * blocks_per_step + blk) * SUBLANES, SUBLANES)
                conv_ref[pl.ds(r0, SUBLANES), cols] = functools.reduce(jnp.add, parts[blk]) + bias
            return carry

        lax.fori_loop(0, ODD_ROWS // CONV_RB, conv_body, 0)
    ypad_ref[0:halo_blocks] = ypad_ref[tile_blocks:tile_blocks + halo_blocks]

    y = jax.nn.silu(_layer_norm(conv_ref[...], lng_ref[...], lnb_ref[...]))
    z = _dot(h, win_ref[:, 2 * d:3 * d])
    out = _dot((y * jax.nn.silu(z)).astype(BF16), wout_ref[...])
    o_ref[0] = x + _rms(out, gpost_ref[...])


def _odd_layer(x3, g_pre, w_in, dw_w, dw_b, ln_g, ln_b, w_out, g_post):
    batch, seq, d = x3.shape
    tile = lambda b, t: (b, t, 0)
    fixed = lambda b, t: (0, 0)
    vec = lambda v: v.reshape(1, d)
    tap = np.arange(CONV_SPAN)[:, None] - np.arange(SUBLANES)[None, :]
    dww = jnp.where(((tap >= 0) & (tap < CONV_K))[:, :, None],
                    dw_w[np.clip(tap, 0, CONV_K - 1)], 0.0)
    return pl.pallas_call(
        _odd_kernel,
        grid=(batch, seq // ODD_ROWS),
        in_specs=[pl.BlockSpec((1, ODD_ROWS, d), tile), pl.BlockSpec((1, d), fixed),
                  pl.BlockSpec((d, 3 * d), fixed),
                  pl.BlockSpec((CONV_SPAN, SUBLANES, d), lambda b, t: (0, 0, 0)),
                  pl.BlockSpec((1, d), fixed), pl.BlockSpec((1, d), fixed), pl.BlockSpec((1, d), fixed),
                  pl.BlockSpec((d, d), fixed), pl.BlockSpec((1, d), fixed)],
        out_specs=pl.BlockSpec((1, ODD_ROWS, d), tile),
        out_shape=jax.ShapeDtypeStruct((batch, seq, d), F32),
        scratch_shapes=[pltpu.VMEM(((CONV_HALO + ODD_ROWS) // SUBLANES, SUBLANES, d), F32),
                        pltpu.VMEM((ODD_ROWS, d), F32)],
        compiler_params=_cparams(("arbitrary", "arbitrary")),
        name="odd_layer",
    )(x3, vec(g_pre), w_in.astype(BF16), dww, vec(dw_b), vec(ln_g), vec(ln_b),
      w_out.astype(BF16), vec(g_post))


def kernel(x, norm_pre, norm_post, even_w_in, even_cmp_k_pe, even_cmp_k_w1, even_cmp_k_w2,
           even_cmp_v_pe, even_cmp_v_w1, even_cmp_v_w2, even_sgu_ln_g, even_sgu_ln_b,
           even_sgu_w, even_sgu_b, even_w_out, odd_w_in, odd_dw_w, odd_dw_b, odd_ln_g,
           odd_ln_b, odd_w_out):
    batch, seq, d = x.shape
    depth = norm_pre.shape[0]
    for i in range(depth):
        li = i // 2
        if i % 2 == 0:
            x2 = x.reshape(batch * seq, d)
            q, kcvc, kvsw, sa, usb, vn, gates = _even_in(
                x2, norm_pre[i], even_w_in[li], even_sgu_ln_g[li], even_sgu_ln_b[li])
            cw = _compress_weights(even_cmp_k_pe[li], even_cmp_k_w1[li], even_cmp_k_w2[li],
                                   even_cmp_v_pe[li], even_cmp_v_w1[li], even_cmp_v_w2[li])
            kcmp, vcmp = _compress(kcvc, batch, seq, cw)
            a = _attention(q, kvsw, kcmp, vcmp, gates, batch, seq)
            x = _even_out(a.reshape(batch * seq, NSA_W), sa, usb, vn, even_sgu_w[li], even_sgu_b[li],
                          even_w_out[li], norm_post[i], x2).reshape(batch, seq, d)
        else:
            x = _odd_layer(x, norm_pre[i], odd_w_in[li], odd_dw_w[li], odd_dw_b[li],
                           odd_ln_g[li], odd_ln_b[li], odd_w_out[li], norm_post[i])
    return x
```

```python
import functools

import jax
import jax.numpy as jnp
import numpy as np
from jax import lax
from jax.experimental import pallas as pl
from jax.experimental.pallas import tpu as pltpu

N_HEADS = 8
N_KV = 2
GROUP = N_HEADS // N_KV
HEAD_DIM = 64
NSA_W = N_HEADS * HEAD_DIM
KV_W = N_KV * HEAD_DIM
CMP_BLOCK = 32
CMP_STRIDE = 16
SLC_BLOCK = 64
SLC_TOPK = 8
WINDOW = 512
FORCE_BONUS = 1e4
NEG_INF = -1e30
TAKEN = -3e38
LOG2E = 1.4426950408889634
SGU_GROUPS = 8
SGU_GROUP_DIM = 64
SGU_W = SGU_GROUPS * SGU_GROUP_DIM
SGU_CHUNK = 128
CONV_K = 31
RMS_EPS = 1e-6
LN_EPS = 1e-5

LANES = 128
SUBLANES = 8
VMEM_LIMIT_BYTES = 56 * 1024 * 1024

IN_ROWS = 512
ATT_Q = 128
ATT_PAST = 256
OUT_ROWS = 512
ODD_ROWS = 512
ODD_COLS = 256
CONV_HALO = 32
CONV_RB = 32
CONV_CHAINS = 2
CONV_SPAN = CONV_K + SUBLANES - 1

F32 = jnp.float32
BF16 = jnp.bfloat16


def _cparams(sem):
    return pltpu.CompilerParams(dimension_semantics=sem, vmem_limit_bytes=VMEM_LIMIT_BYTES)


def _rms(x, g):
    return x * lax.rsqrt(jnp.mean(x * x, axis=-1, keepdims=True) + RMS_EPS) * g


def _layer_norm(x, g, b):
    mu = jnp.mean(x, axis=-1, keepdims=True)
    xc = x - mu
    return xc * lax.rsqrt(jnp.mean(xc * xc, axis=-1, keepdims=True) + LN_EPS) * g + b


def _dot(a, b):
    return jnp.dot(a, b, preferred_element_type=F32)


EV_Q, EV_KC, EV_KS, EV_ZA, EV_U, EV_V, EV_ZB, EV_G, EV_END = 0, 512, 768, 1280, 1792, 2304, 2816, 3328, 3456


def _even_in_kernel(x_ref, g_ref, w_ref, lng_ref, lnb_ref,
                    q_ref, kc_ref, vc_ref, kvsw_ref, sa_ref, usb_ref, vn_ref, gate_ref):
    h = _rms(x_ref[...], g_ref[...]).astype(BF16)
    q_ref[...] = (_dot(h, w_ref[:, EV_Q:EV_KC]) * (HEAD_DIM ** -0.5 * LOG2E)).astype(BF16)
    kcvc = _dot(h, w_ref[:, EV_KC:EV_KS])
    kc_ref[...] = kcvc[:, :KV_W]
    vc_ref[...] = kcvc[:, KV_W:]
    kvsw_ref[...] = _dot(h, w_ref[:, EV_KS:EV_ZA]).astype(BF16)
    sa_ref[...] = jax.nn.silu(_dot(h, w_ref[:, EV_ZA:EV_U])).astype(BF16)
    u = _dot(h, w_ref[:, EV_U:EV_V])
    zb = _dot(h, w_ref[:, EV_ZB:EV_G])
    usb_ref[...] = (jax.nn.gelu(u) * jax.nn.silu(zb)).astype(BF16)
    v = jax.nn.gelu(_dot(h, w_ref[:, EV_V:EV_ZB]))
    vn_ref[...] = _layer_norm(v, lng_ref[...], lnb_ref[...]).astype(BF16)
    gate_ref[...] = jax.nn.sigmoid(_dot(h, w_ref[:, EV_G:EV_END]))


def _even_in(x2, g_pre, w_in, ln_g, ln_b):
    t, d = x2.shape
    wb = w_in.astype(BF16)
    w = jnp.concatenate(
        [wb[:, :1280], wb[:, 1304:], wb[:, 1280:1304], jnp.zeros((d, LANES - 3 * N_HEADS), BF16)], axis=1)
    row = lambda i: (i, 0)
    fixed = lambda i: (0, 0)
    outs = [(NSA_W, BF16), (KV_W, F32), (KV_W, F32), (4 * KV_W, BF16), (NSA_W, BF16),
            (SGU_W, BF16), (SGU_W, BF16), (LANES, F32)]
    return pl.pallas_call(
        _even_in_kernel,
        grid=(t // IN_ROWS,),
        in_specs=[pl.BlockSpec((IN_ROWS, d), row), pl.BlockSpec((1, d), fixed),
                  pl.BlockSpec((d, EV_END), fixed), pl.BlockSpec((1, SGU_W), fixed),
                  pl.BlockSpec((1, SGU_W), fixed)],
        out_specs=[pl.BlockSpec((IN_ROWS, n), row) for n, _ in outs],
        out_shape=[jax.ShapeDtypeStruct((t, n), dt) for n, dt in outs],
        compiler_params=_cparams(("arbitrary",)),
        name="even_in",
    )(x2, g_pre.reshape(1, d), w, ln_g.reshape(1, SGU_W), ln_b.reshape(1, SGU_W))


def _compress_kernel(kc_ref, vc_ref, pe_ref, w1_ref, w2_ref, kcmp_ref, vcmp_ref):
    n_rows = kcmp_ref.shape[1]
    rows = lax.broadcasted_iota(jnp.int32, (n_rows, KV_W), 0)
    for ti, (src_ref, dst_ref) in enumerate(((kc_ref, kcmp_ref), (vc_ref, vcmp_ref))):
        kr = jnp.concatenate([src_ref[0, pl.ds(l, n_rows, stride=CMP_STRIDE), :]
                              for l in range(CMP_STRIDE)], axis=1)
        first = _dot((kr + pe_ref[ti, 0]).astype(BF16), w1_ref[ti, 0])
        second = _dot((kr + pe_ref[ti, 1]).astype(BF16), w1_ref[ti, 1])
        pre = first + pltpu.roll(second, n_rows - 1, 0)
        out = _dot(jax.nn.silu(pre).astype(BF16), w2_ref[ti])
        dst_ref[0] = jnp.where(rows < n_rows - 1, out, 0.0).astype(BF16)


def _compress_weights(k_pe, k_w1, k_w2, v_pe, v_w1, v_w2):
    dh, half = HEAD_DIM, CMP_STRIDE
    eye_g = jnp.eye(N_KV, dtype=F32)

    def first_layer(w1, lo):
        w = w1.reshape(CMP_BLOCK, dh, dh)[lo:lo + half]
        return jnp.einsum('lde,gh->lgdhe', w, eye_g).reshape(half * KV_W, KV_W)

    def pe_row(pe, lo):
        return jnp.tile(pe[lo:lo + half], (1, N_KV)).reshape(1, half * KV_W)

    pe = jnp.stack([jnp.stack([pe_row(p, 0), pe_row(p, half)]) for p in (k_pe, v_pe)])
    w1 = jnp.stack([jnp.stack([first_layer(w, 0), first_layer(w, half)]) for w in (k_w1, v_w1)])
    w2 = jnp.stack([jnp.einsum('de,gh->gdhe', w, eye_g).reshape(KV_W, KV_W) for w in (k_w2, v_w2)])
    return pe, w1.astype(BF16), w2.astype(BF16)


def _compress(kc, vc, batch, seq, cw):
    pe, w1, w2 = cw
    n_rows = seq // CMP_STRIDE
    width = CMP_STRIDE * KV_W
    per_b = lambda b: (b, 0, 0)
    return pl.pallas_call(
        _compress_kernel,
        grid=(batch,),
        in_specs=[pl.BlockSpec((1, seq, KV_W), per_b), pl.BlockSpec((1, seq, KV_W), per_b),
                  pl.BlockSpec((2, 2, 1, width), lambda b: (0, 0, 0, 0)),
                  pl.BlockSpec((2, 2, width, KV_W), lambda b: (0, 0, 0, 0)),
                  pl.BlockSpec((2, KV_W, KV_W), lambda b: (0, 0, 0))],
        out_specs=[pl.BlockSpec((1, n_rows, KV_W), per_b)] * 2,
        out_shape=[jax.ShapeDtypeStruct((batch, n_rows, KV_W), BF16)] * 2,
        compiler_params=_cparams(("arbitrary",)),
        name="compress",
    )(kc.reshape(batch, seq, KV_W), vc.reshape(batch, seq, KV_W), pe, w1, w2)


def _overlap_matrix(n_rows, n_blk):
    n_cmp = n_rows - 1
    tok = np.arange(n_cmp)[:, None] * CMP_STRIDE + np.arange(CMP_BLOCK)[None, :]
    ov = ((tok[:, :, None] // SLC_BLOCK) == np.arange(n_blk)[None, None, :]).sum(1)
    ov = ov.astype(np.float32) / np.float32(CMP_BLOCK)
    out = np.zeros((N_KV, LANES, n_rows), np.float32)
    for g in range(N_KV):
        other = (1 - g) * HEAD_DIM
        out[g, other:other + n_blk, :n_cmp] = ov.T
    return out


def _attn_kernel(q_ref, kv_ref, kcmp_ref, vcmp_ref, gate_ref, ovt_ref, o_ref,
                 ksel_ref, kwin_ref, vselt_ref, vwint_ref, vcmpt_ref, qsel_ref, acc_ref,
                 sc_ref, pr_ref, *, seq):
    tq = ATT_Q
    pw = 2 * tq
    kc = ATT_PAST
    n_blk = seq // SLC_BLOCK
    w_len = WINDOW + tq
    t = pl.program_id(1)
    t0 = pl.multiple_of(t * tq, tq)
    sub = lax.broadcasted_iota(jnp.int32, (LANES, pw), 0)
    tok_off = lax.broadcasted_iota(jnp.int32, (LANES, pw), 1) & (tq - 1)
    causal = sub <= tok_off
    zero_bf = jnp.zeros((), BF16)

    @pl.when(t == 0)
    def _build_keys():
        key_blk = lax.broadcasted_iota(jnp.int32, (seq, LANES), 0) // SLC_BLOCK
        lane_s = lax.broadcasted_iota(jnp.int32, (seq, LANES), 1)
        lane_w = lax.broadcasted_iota(jnp.int32, (WINDOW, LANES), 1)
        vs_t = kv_ref[0, :, KV_W:2 * KV_W].astype(F32).T
        vw_t = kv_ref[0, :, 3 * KV_W:4 * KV_W].astype(F32).T
        vc_t = vcmp_ref[0].astype(F32).T
        for g in range(N_KV):
            other = (1 - g) * HEAD_DIM
            mine = (lane_s // HEAD_DIM) == g
            onehot = jnp.where(lane_s - other == key_blk, 1.0, 0.0).astype(BF16)
            ksel_ref[g] = jnp.where(mine, kv_ref[0, :, 0:KV_W], onehot)
            kwin_ref[g, WINDOW:, :] = jnp.where(mine, kv_ref[0, :, 2 * KV_W:3 * KV_W], zero_bf)
            kwin_ref[g, 0:WINDOW, :] = jnp.where(lane_w == other, 1.0, 0.0).astype(BF16)

            def with_ones(v_t):
                row = lax.broadcasted_iota(jnp.int32, v_t.shape, 0)
                return jnp.where(row // HEAD_DIM == g, v_t, jnp.where(row == other, 1.0, 0.0)).astype(BF16)

            vselt_ref[g] = with_ones(vs_t)
            vwint_ref[g, :, 0:WINDOW] = with_ones(jnp.zeros((LANES, WINDOW), F32))
            vwint_ref[g, :, WINDOW:] = with_ones(vw_t)
            vcmpt_ref[g] = with_ones(vc_t)

    gates_t = gate_ref[...].T

    def gate_row(branch, h0):
        r = branch * N_HEADS + h0
        return jnp.concatenate([gates_t[r:r + 1], gates_t[r + 1:r + 2]], axis=1)

    pairs = range(N_HEADS // 2)
    group_of = [pi // (GROUP // 2) for pi in pairs]
    other_of = [(1 - g) * HEAD_DIM for g in range(N_KV)]
    mine_rows = [(sub // HEAD_DIM) == g for g in range(N_KV)]

    def normalised(o_t, pi):
        g = group_of[pi]
        return o_t[g * HEAD_DIM:(g + 1) * HEAD_DIM] / o_t[other_of[g]:other_of[g] + 1]


    qts, q_plain, q_mark = [], [], []
    for pi in pairs:
        g = group_of[pi]
        raw_t = q_ref[0, :, pi * LANES:(pi + 1) * LANES].astype(F32).T
        swap_t = jnp.concatenate([raw_t[HEAD_DIM:], raw_t[:HEAD_DIM]], axis=0)
        even_t, odd_t = (raw_t, swap_t) if g == 0 else (swap_t, raw_t)
        q_t = jnp.concatenate([even_t, odd_t], axis=1)
        qts.append(q_t)
        q_plain.append(jnp.where(mine_rows[g], q_t, 0.0).astype(BF16))
        marker = jnp.where(sub == other_of[g], NEG_INF, 0.0)
        q_mark.append(jnp.where(mine_rows[g], q_t, marker).astype(BF16))

    s_cmp = [_dot(kcmp_ref[0], q_plain[pi]) for pi in pairs]
    s_win = [_dot(kwin_ref[group_of[pi], pl.ds(t0, w_len), :], q_mark[pi]) for pi in pairs]
    s_dg = [_dot(ksel_ref[group_of[pi], pl.ds(t0, tq), :], q_plain[pi]) for pi in pairs]

    valid = sub * CMP_STRIDE + (CMP_BLOCK - 1) <= t0 + tok_off
    p_cmp = []
    for pi in pairs:
        s = jnp.where(valid, s_cmp[pi], NEG_INF)
        e = jnp.exp2(s - jnp.max(s, axis=0, keepdims=True))
        p_cmp.append(jnp.where(valid, e / jnp.sum(e, axis=0, keepdims=True), 0.0))
    o_cmps = [_dot(vcmpt_ref[group_of[pi]], p_cmp[pi].astype(BF16))[group_of[pi] * HEAD_DIM:
                                                                      (group_of[pi] + 1) * HEAD_DIM]
              for pi in pairs]
    imp_ts = []
    for g in range(N_KV):
        psum = functools.reduce(jnp.add, [p_cmp[pi][:, :tq] + p_cmp[pi][:, tq:]
                                          for pi in pairs if group_of[pi] == g])
        imp_ts.append(jnp.dot(ovt_ref[g], psum, preferred_element_type=F32,
                              precision=lax.Precision.HIGHEST))

    p_win = []
    for pi in pairs:
        sw = s_win[pi]
        parts = [jnp.where(sub > tok_off, sw[0:tq], NEG_INF), sw[tq:WINDOW],
                 jnp.where(causal, sw[WINDOW:], NEG_INF)]
        m = functools.reduce(jnp.maximum, [jnp.max(s, axis=0, keepdims=True) for s in parts])
        p_win.append(jnp.concatenate([jnp.exp2(s - m).astype(BF16) for s in parts], axis=0))
    o_wins = [normalised(_dot(vwint_ref[group_of[pi], :, pl.ds(t0, w_len)], p_win[pi]), pi) for pi in pairs]

    blk_id = lax.broadcasted_iota(jnp.int32, (n_blk, tq), 0)
    blk_f = blk_id.astype(F32)
    cur = (t0 + lax.broadcasted_iota(jnp.int32, (n_blk, tq), 1)) // SLC_BLOCK
    forced = (blk_id == 0) | (blk_id == cur) | (blk_id == cur - 1)
    for g in range(N_KV):
        other = other_of[g]
        imp = imp_ts[g][other:other + n_blk, :]
        imp = jnp.where(blk_id <= cur, imp + jnp.where(forced, FORCE_BONUS, 0.0), NEG_INF)
        chosen = jnp.zeros((n_blk, tq), F32)
        for _ in range(min(SLC_TOPK, n_blk)):
            top = jnp.max(imp, axis=0, keepdims=True)
            first = jnp.min(jnp.where(imp == top, blk_f, float(n_blk)), axis=0, keepdims=True)
            hit = blk_f == first
            chosen = jnp.where(hit, 1.0, chosen)
            imp = jnp.where(hit, TAKEN, imp)
        keep = (chosen > 0.5) & (blk_id < t * (tq // SLC_BLOCK))
        pen = jnp.where(keep, 0.0, NEG_INF)
        pen = jnp.concatenate(
            [jnp.zeros((n, tq), F32) for n in (other,) if n] + [pen]
            + [jnp.zeros((n, tq), F32) for n in (LANES - other - n_blk,) if n], axis=0)
        pen = jnp.concatenate([pen, pen], axis=1)
        for pi in pairs:
            if group_of[pi] == g:
                qsel_ref[pi] = jnp.where(mine_rows[g], qts[pi], pen).astype(BF16)

    m0, p_dg = [], []
    for pi in pairs:
        s_d = jnp.where(causal, s_dg[pi], NEG_INF)
        m_d = jnp.max(s_d, axis=0, keepdims=True)
        m0.append(m_d)
        p_dg.append(jnp.exp2(s_d - m_d).astype(BF16))
    for pi in pairs:
        acc_ref[pi] = _dot(vselt_ref[group_of[pi], :, pl.ds(t0, tq)], p_dg[pi])

    n_past = (t * tq + kc - 1) // kc
    max_chunk = seq // kc - 1

    def chunk_start(c):
        return pl.multiple_of(jnp.clip(c, 0, max_chunk) * kc, kc)

    def scores_into(c, slot):
        k0 = chunk_start(c)
        for pi in pairs:
            sc_ref[slot, pi] = _dot(ksel_ref[group_of[pi], pl.ds(k0, kc), :], qsel_ref[pi])

    def values_from(c, slot, alphas):
        k0 = chunk_start(c)
        for pi in pairs:
            acc_ref[pi] = alphas[pi] * acc_ref[pi] + _dot(vselt_ref[group_of[pi], :, pl.ds(k0, kc)],
                                                          pr_ref[slot, pi])

    def softmax_slot(slot, ms):
        new_m, alphas = [], []
        for pi in pairs:
            s = sc_ref[slot, pi]
            m_new = jnp.maximum(ms[pi], jnp.max(s, axis=0, keepdims=True))
            alphas.append(jnp.exp2(ms[pi] - m_new))
            pr_ref[slot, pi] = jnp.exp2(s - m_new).astype(BF16)
            new_m.append(m_new)
        return tuple(new_m), tuple(alphas)

    def past_body(i, carry):
        ms, alphas = carry
        c0 = 2 * i
        scores_into(c0 + 1, 1)
        values_from(c0 - 1, 1, alphas)
        ms, alphas = softmax_slot(0, ms)
        scores_into(c0 + 2, 0)
        values_from(c0, 0, alphas)
        return softmax_slot(1, ms)

    scores_into(0, 0)
    for pi in pairs:
        pr_ref[1, pi] = jnp.zeros((kc, pw), BF16)
    n_trips = (n_past + 1) // 2
    _, alphas = lax.fori_loop(0, n_trips, past_body,
                              (tuple(m0), tuple(jnp.ones((1, pw), F32) for _ in pairs)))
    values_from(2 * n_trips - 1, 1, alphas)

    head_out = [None] * N_HEADS
    for pi in pairs:
        h0 = 2 * pi
        mixed = (gate_row(0, h0) * o_cmps[pi] + gate_row(1, h0) * normalised(acc_ref[pi], pi)
                 + gate_row(2, h0) * o_wins[pi])
        head_out[h0] = mixed[:, :tq]
        head_out[h0 + 1] = mixed[:, tq:]

    o_ref[0] = jnp.concatenate(head_out, axis=0).T.astype(BF16)


def _attention(q, kvsw, kcmp, vcmp, gates, batch, seq):
    n_rows = seq // CMP_STRIDE
    n_pairs = N_HEADS // 2
    ovt = jnp.asarray(_overlap_matrix(n_rows, seq // SLC_BLOCK))
    tile = lambda b, t: (b, t, 0)
    per_b = lambda b, t: (b, 0, 0)
    return pl.pallas_call(
        functools.partial(_attn_kernel, seq=seq),
        grid=(batch, seq // ATT_Q),
        in_specs=[pl.BlockSpec((1, ATT_Q, NSA_W), tile),
                  pl.BlockSpec((1, seq, 4 * KV_W), per_b),
                  pl.BlockSpec((1, n_rows, KV_W), per_b),
                  pl.BlockSpec((1, n_rows, KV_W), per_b),
                  pl.BlockSpec((ATT_Q, LANES), lambda b, t: (b * (seq // ATT_Q) + t, 0)),
                  pl.BlockSpec((N_KV, LANES, n_rows), lambda b, t: (0, 0, 0))],
        out_specs=pl.BlockSpec((1, ATT_Q, NSA_W), tile),
        out_shape=jax.ShapeDtypeStruct((batch, seq, NSA_W), BF16),
        scratch_shapes=[pltpu.VMEM((N_KV, seq, LANES), BF16),
                        pltpu.VMEM((N_KV, WINDOW + seq, LANES), BF16),
                        pltpu.VMEM((N_KV, LANES, seq), BF16),
                        pltpu.VMEM((N_KV, LANES, WINDOW + seq), BF16),
                        pltpu.VMEM((N_KV, LANES, n_rows), BF16),
                        pltpu.VMEM((n_pairs, LANES, 2 * ATT_Q), BF16),
                        pltpu.VMEM((n_pairs, LANES, 2 * ATT_Q), F32),
                        pltpu.VMEM((2, n_pairs, ATT_PAST, 2 * ATT_Q), F32),
                        pltpu.VMEM((2, n_pairs, ATT_PAST, 2 * ATT_Q), BF16)],
        compiler_params=_cparams(("arbitrary", "arbitrary")),
        name="nsa_attention",
    )(q.reshape(batch, seq, NSA_W), kvsw.reshape(batch, seq, 4 * KV_W), kcmp, vcmp, gates, ovt)


def _even_out_kernel(a_ref, sa_ref, usb_ref, vn_ref, ws_ref, bias_ref, w_ref, g_ref, x_ref, o_ref):
    lane = lax.broadcasted_iota(jnp.int32, (SGU_CHUNK, LANES), 1)
    low = lane < SGU_GROUP_DIM
    ri = lax.broadcasted_iota(jnp.int32, (SGU_CHUNK, SGU_CHUNK), 0)
    ci = lax.broadcasted_iota(jnp.int32, (SGU_CHUNK, SGU_CHUNK), 1)
    tril = ci <= ri
    wmix = [jnp.where(tril, ws_ref[g], 0.0).astype(BF16) for g in range(SGU_GROUPS)]
    zero = jnp.zeros((), BF16)
    left = (a_ref[...].astype(F32) * sa_ref[...].astype(F32)).astype(BF16)
    rights = []
    for c in range(OUT_ROWS // SGU_CHUNK):
        rows = slice(c * SGU_CHUNK, (c + 1) * SGU_CHUNK)
        blocks = []
        for p in range(SGU_W // LANES):
            cols = slice(p * LANES, (p + 1) * LANES)
            vb = vn_ref[rows, cols]
            mixed = (_dot(wmix[2 * p], jnp.where(low, vb, zero))
                     + _dot(wmix[2 * p + 1], jnp.where(low, zero, vb))
                     + bias_ref[:, cols])
            blocks.append((usb_ref[rows, cols].astype(F32) * mixed).astype(BF16))
        rights.append(jnp.concatenate(blocks, axis=1))
    right = jnp.concatenate(rights, axis=0)
    y = _dot(left, w_ref[0:NSA_W, :]) + _dot(right, w_ref[NSA_W:, :])
    o_ref[...] = x_ref[...] + _rms(y, g_ref[...])


def _even_out(a, sa, usb, vn, sgu_w, sgu_b, w_out, g_post, x2):
    t, d = x2.shape
    bias = jnp.repeat(sgu_b.T, SGU_GROUP_DIM, axis=1)
    row = lambda i: (i, 0)
    fixed = lambda i: (0, 0)
    return pl.pallas_call(
        _even_out_kernel,
        grid=(t // OUT_ROWS,),
        in_specs=[pl.BlockSpec((OUT_ROWS, NSA_W), row), pl.BlockSpec((OUT_ROWS, NSA_W), row),
                  pl.BlockSpec((OUT_ROWS, SGU_W), row), pl.BlockSpec((OUT_ROWS, SGU_W), row),
                  pl.BlockSpec((SGU_GROUPS, SGU_CHUNK, SGU_CHUNK), lambda i: (0, 0, 0)),
                  pl.BlockSpec((SGU_CHUNK, SGU_W), fixed),
                  pl.BlockSpec((NSA_W + SGU_W, d), fixed), pl.BlockSpec((1, d), fixed),
                  pl.BlockSpec((OUT_ROWS, d), row)],
        out_specs=pl.BlockSpec((OUT_ROWS, d), row),
        out_shape=jax.ShapeDtypeStruct((t, d), F32),
        compiler_params=_cparams(("arbitrary",)),
        name="even_out",
    )(a, sa, usb, vn, sgu_w, bias, w_out.astype(BF16), g_post.reshape(1, d), x2)


def _odd_kernel(x_ref, gpre_ref, win_ref, dww_ref, dwb_ref, lng_ref, lnb_ref, wout_ref, gpost_ref,
                o_ref, ypad_ref, conv_ref, sz_ref):
    d = x_ref.shape[-1]
    x = x_ref[0]
    h = _rms(x, gpre_ref[...]).astype(BF16)
    n_slices = d // ODD_COLS

    @pl.when(pl.program_id(1) == 0)
    def _():
        ypad_ref[0:CONV_HALO, :] = jnp.zeros((CONV_HALO, d), F32)

    def project_glu(cs):
        c0, c1 = cs * ODD_COLS, (cs + 1) * ODD_COLS
        a = _dot(h, win_ref[:, c0:c1])
        gl = _dot(h, win_ref[:, d + c0:d + c1])
        ypad_ref[CONV_HALO:, c0:c1] = a * jax.nn.sigmoid(gl)

    def project_gate(cs):
        c0, c1 = cs * ODD_COLS, (cs + 1) * ODD_COLS
        sz_ref[:, c0:c1] = jax.nn.silu(_dot(h, win_ref[:, 2 * d + c0:2 * d + c1])).astype(BF16)

    shift = CONV_HALO - (CONV_K - 1)
    blocks_per_step = CONV_RB // SUBLANES

    def conv(cs):
        for lt in range(cs * ODD_COLS // LANES, (cs + 1) * ODD_COLS // LANES):
            cols = slice(lt * LANES, (lt + 1) * LANES)
            wt = [dww_ref[j, :, cols] for j in range(CONV_SPAN)]
            bias = jnp.broadcast_to(dwb_ref[:, cols], (SUBLANES, LANES))
            for i in range(ODD_ROWS // CONV_RB):
                parts = [[None] * CONV_CHAINS for _ in range(blocks_per_step)]
                for j in range(CONV_SPAN + CONV_RB - SUBLANES):
                    src = i * CONV_RB + shift + j
                    row = jnp.broadcast_to(ypad_ref[src:src + 1, cols], (SUBLANES, LANES))
                    for blk in range(blocks_per_step):
                        jj = j - blk * SUBLANES
                        if 0 <= jj < CONV_SPAN:
                            term = wt[jj] * row
                            old = parts[blk][jj % CONV_CHAINS]
                            parts[blk][jj % CONV_CHAINS] = term if old is None else old + term
                for blk in range(blocks_per_step):
                    r0 = (i * blocks_per_step + blk) * SUBLANES
                    conv_ref[r0:r0 + SUBLANES, cols] = functools.reduce(jnp.add, parts[blk]) + bias

    project_glu(0)
    for cs in range(n_slices):
        if cs + 1 < n_slices:
            project_glu(cs + 1)
        else:
            for gs in range(n_slices):
                project_gate(gs)
        conv(cs)
    ypad_ref[0:CONV_HALO, :] = ypad_ref[ODD_ROWS:ODD_ROWS + CONV_HALO, :]

    y = jax.nn.silu(_layer_norm(conv_ref[...], lng_ref[...], lnb_ref[...]))
    out = _dot((y * sz_ref[...].astype(F32)).astype(BF16), wout_ref[...])
    o_ref[0] = x + _rms(out, gpost_ref[...])


def _odd_layer(x3, g_pre, w_in, dw_w, dw_b, ln_g, ln_b, w_out, g_post):
    batch, seq, d = x3.shape
    tile = lambda b, t: (b, t, 0)
    fixed = lambda b, t: (0, 0)
    vec = lambda v: v.reshape(1, d)
    tap = np.arange(CONV_SPAN)[:, None] - np.arange(SUBLANES)[None, :]
    dww = jnp.where(((tap >= 0) & (tap < CONV_K))[:, :, None],
                    dw_w[np.clip(tap, 0, CONV_K - 1)], 0.0)
    return pl.pallas_call(
        _odd_kernel,
        grid=(batch, seq // ODD_ROWS),
        in_specs=[pl.BlockSpec((1, ODD_ROWS, d), tile), pl.BlockSpec((1, d), fixed),
                  pl.BlockSpec((d, 3 * d), fixed),
                  pl.BlockSpec((CONV_SPAN, SUBLANES, d), lambda b, t: (0, 0, 0)),
                  pl.BlockSpec((1, d), fixed), pl.BlockSpec((1, d), fixed), pl.BlockSpec((1, d), fixed),
                  pl.BlockSpec((d, d), fixed), pl.BlockSpec((1, d), fixed)],
        out_specs=pl.BlockSpec((1, ODD_ROWS, d), tile),
        out_shape=jax.ShapeDtypeStruct((batch, seq, d), F32),
        scratch_shapes=[pltpu.VMEM((CONV_HALO + ODD_ROWS, d), F32),
                        pltpu.VMEM((ODD_ROWS, d), F32), pltpu.VMEM((ODD_ROWS, d), BF16)],
        compiler_params=_cparams(("arbitrary", "arbitrary")),
        name="odd_layer",
    )(x3, vec(g_pre), w_in.astype(BF16), dww, vec(dw_b), vec(ln_g), vec(ln_b),
      w_out.astype(BF16), vec(g_post))


def kernel(x, norm_pre, norm_post, even_w_in, even_cmp_k_pe, even_cmp_k_w1, even_cmp_k_w2,
           even_cmp_v_pe, even_cmp_v_w1, even_cmp_v_w2, even_sgu_ln_g, even_sgu_ln_b,
           even_sgu_w, even_sgu_b, even_w_out, odd_w_in, odd_dw_w, odd_dw_b, odd_ln_g,
           odd_ln_b, odd_w_out):
    batch, seq, d = x.shape
    depth = norm_pre.shape[0]
    for i in range(depth):
        li = i // 2
        if i % 2 == 0:
            x2 = x.reshape(batch * seq, d)
            q, kc, vc, kvsw, sa, usb, vn, gates = _even_in(
                x2, norm_pre[i], even_w_in[li], even_sgu_ln_g[li], even_sgu_ln_b[li])
            cw = _compress_weights(even_cmp_k_pe[li], even_cmp_k_w1[li], even_cmp_k_w2[li],
                                   even_cmp_v_pe[li], even_cmp_v_w1[li], even_cmp_v_w2[li])
            kcmp, vcmp = _compress(kc, vc, batch, seq, cw)
            a = _attention(q, kvsw, kcmp, vcmp, gates, batch, seq)
            x = _even_out(a.reshape(batch * seq, NSA_W), sa, usb, vn, even_sgu_w[li], even_sgu_b[li],
                          even_w_out[li], norm_post[i], x2).reshape(batch, seq, d)
        else:
            x = _odd_layer(x, norm_pre[i], odd_w_in[li], odd_dw_w[li], odd_dw_b[li],
                           odd_ln_g[li], odd_ln_b[li], odd_w_out[li], norm_post[i])
    return x
```

```python
import functools

import jax
import jax.numpy as jnp
import numpy as np
from jax import lax
from jax.experimental import pallas as pl
from jax.experimental.pallas import tpu as pltpu

N_HEADS = 8
N_KV = 2
GROUP = N_HEADS // N_KV
HEAD_DIM = 64
NSA_W = N_HEADS * HEAD_DIM
KV_W = N_KV * HEAD_DIM
CMP_BLOCK = 32
CMP_STRIDE = 16
SLC_BLOCK = 64
SLC_TOPK = 8
WINDOW = 512
FORCE_BONUS = 1e4
NEG_INF = -1e30
TAKEN = -3e38
LOG2E = 1.4426950408889634
SGU_GROUPS = 8
SGU_GROUP_DIM = 64
SGU_W = SGU_GROUPS * SGU_GROUP_DIM
SGU_CHUNK = 128
CONV_K = 31
RMS_EPS = 1e-6
LN_EPS = 1e-5

LANES = 128
SUBLANES = 8
VMEM_LIMIT_BYTES = 56 * 1024 * 1024

IN_ROWS = 512
ATT_Q = 128
ATT_PAST = 256
OUT_ROWS = 512
ODD_ROWS = 512
ODD_COLS = 256
CONV_HALO = 32
CONV_RB = 64
CONV_CHAINS = 1

F32 = jnp.float32
BF16 = jnp.bfloat16


def _cparams(sem):
    return pltpu.CompilerParams(dimension_semantics=sem, vmem_limit_bytes=VMEM_LIMIT_BYTES)


def _rms(x, g):
    return x * lax.rsqrt(jnp.mean(x * x, axis=-1, keepdims=True) + RMS_EPS) * g


def _layer_norm(x, g, b):
    mu = jnp.mean(x, axis=-1, keepdims=True)
    xc = x - mu
    return xc * lax.rsqrt(jnp.mean(xc * xc, axis=-1, keepdims=True) + LN_EPS) * g + b


def _dot(a, b):
    return jnp.dot(a, b, preferred_element_type=F32)


EV_Q, EV_KC, EV_KS, EV_ZA, EV_U, EV_V, EV_ZB, EV_G, EV_END = 0, 512, 768, 1280, 1792, 2304, 2816, 3328, 3456


def _even_in_kernel(x_ref, g_ref, w_ref, lng_ref, lnb_ref,
                    q_ref, kc_ref, vc_ref, kvsw_ref, sa_ref, usb_ref, vn_ref, gate_ref):
    h = _rms(x_ref[...], g_ref[...]).astype(BF16)
    q_ref[...] = (_dot(h, w_ref[:, EV_Q:EV_KC]) * (HEAD_DIM ** -0.5 * LOG2E)).astype(BF16)
    kcvc = _dot(h, w_ref[:, EV_KC:EV_KS])
    kc_ref[...] = kcvc[:, :KV_W]
    vc_ref[...] = kcvc[:, KV_W:]
    kvsw_ref[...] = _dot(h, w_ref[:, EV_KS:EV_ZA]).astype(BF16)
    sa_ref[...] = jax.nn.silu(_dot(h, w_ref[:, EV_ZA:EV_U])).astype(BF16)
    u = _dot(h, w_ref[:, EV_U:EV_V])
    zb = _dot(h, w_ref[:, EV_ZB:EV_G])
    usb_ref[...] = (jax.nn.gelu(u) * jax.nn.silu(zb)).astype(BF16)
    v = jax.nn.gelu(_dot(h, w_ref[:, EV_V:EV_ZB]))
    vn_ref[...] = _layer_norm(v, lng_ref[...], lnb_ref[...]).astype(BF16)
    gate_ref[...] = jax.nn.sigmoid(_dot(h, w_ref[:, EV_G:EV_END]))


def _even_in(x2, g_pre, w_in, ln_g, ln_b):
    t, d = x2.shape
    wb = w_in.astype(BF16)
    w = jnp.concatenate(
        [wb[:, :1280], wb[:, 1304:], wb[:, 1280:1304], jnp.zeros((d, LANES - 3 * N_HEADS), BF16)], axis=1)
    row = lambda i: (i, 0)
    fixed = lambda i: (0, 0)
    outs = [(NSA_W, BF16), (KV_W, F32), (KV_W, F32), (4 * KV_W, BF16), (NSA_W, BF16),
            (SGU_W, BF16), (SGU_W, BF16), (LANES, F32)]
    return pl.pallas_call(
        _even_in_kernel,
        grid=(t // IN_ROWS,),
        in_specs=[pl.BlockSpec((IN_ROWS, d), row), pl.BlockSpec((1, d), fixed),
                  pl.BlockSpec((d, EV_END), fixed), pl.BlockSpec((1, SGU_W), fixed),
                  pl.BlockSpec((1, SGU_W), fixed)],
        out_specs=[pl.BlockSpec((IN_ROWS, n), row) for n, _ in outs],
        out_shape=[jax.ShapeDtypeStruct((t, n), dt) for n, dt in outs],
        compiler_params=_cparams(("arbitrary",)),
        name="even_in",
    )(x2, g_pre.reshape(1, d), w, ln_g.reshape(1, SGU_W), ln_b.reshape(1, SGU_W))


def _compress_kernel(kc_ref, vc_ref, pe_ref, w1_ref, w2_ref, kcmp_ref, vcmp_ref):
    n_rows = kcmp_ref.shape[1]
    rows = lax.broadcasted_iota(jnp.int32, (n_rows, KV_W), 0)
    for ti, (src_ref, dst_ref) in enumerate(((kc_ref, kcmp_ref), (vc_ref, vcmp_ref))):
        kr = jnp.concatenate([src_ref[0, pl.ds(l, n_rows, stride=CMP_STRIDE), :]
                              for l in range(CMP_STRIDE)], axis=1)
        first = _dot((kr + pe_ref[ti, 0]).astype(BF16), w1_ref[ti, 0])
        second = _dot((kr + pe_ref[ti, 1]).astype(BF16), w1_ref[ti, 1])
        pre = first + pltpu.roll(second, n_rows - 1, 0)
        out = _dot(jax.nn.silu(pre).astype(BF16), w2_ref[ti])
        dst_ref[0] = jnp.where(rows < n_rows - 1, out, 0.0).astype(BF16)


def _compress_weights(k_pe, k_w1, k_w2, v_pe, v_w1, v_w2):
    dh, half = HEAD_DIM, CMP_STRIDE
    eye_g = jnp.eye(N_KV, dtype=F32)

    def first_layer(w1, lo):
        w = w1.reshape(CMP_BLOCK, dh, dh)[lo:lo + half]
        return jnp.einsum('lde,gh->lgdhe', w, eye_g).reshape(half * KV_W, KV_W)

    def pe_row(pe, lo):
        return jnp.tile(pe[lo:lo + half], (1, N_KV)).reshape(1, half * KV_W)

    pe = jnp.stack([jnp.stack([pe_row(p, 0), pe_row(p, half)]) for p in (k_pe, v_pe)])
    w1 = jnp.stack([jnp.stack([first_layer(w, 0), first_layer(w, half)]) for w in (k_w1, v_w1)])
    w2 = jnp.stack([jnp.einsum('de,gh->gdhe', w, eye_g).reshape(KV_W, KV_W) for w in (k_w2, v_w2)])
    return pe, w1.astype(BF16), w2.astype(BF16)


def _compress(kc, vc, batch, seq, cw):
    pe, w1, w2 = cw
    n_rows = seq // CMP_STRIDE
    width = CMP_STRIDE * KV_W
    per_b = lambda b: (b, 0, 0)
    return pl.pallas_call(
        _compress_kernel,
        grid=(batch,),
        in_specs=[pl.BlockSpec((1, seq, KV_W), per_b), pl.BlockSpec((1, seq, KV_W), per_b),
                  pl.BlockSpec((2, 2, 1, width), lambda b: (0, 0, 0, 0)),
                  pl.BlockSpec((2, 2, width, KV_W), lambda b: (0, 0, 0, 0)),
                  pl.BlockSpec((2, KV_W, KV_W), lambda b: (0, 0, 0))],
        out_specs=[pl.BlockSpec((1, n_rows, KV_W), per_b)] * 2,
        out_shape=[jax.ShapeDtypeStruct((batch, n_rows, KV_W), BF16)] * 2,
        compiler_params=_cparams(("arbitrary",)),
        name="compress",
    )(kc.reshape(batch, seq, KV_W), vc.reshape(batch, seq, KV_W), pe, w1, w2)


def _overlap_matrix(n_rows, n_blk):
    n_cmp = n_rows - 1
    tok = np.arange(n_cmp)[:, None] * CMP_STRIDE + np.arange(CMP_BLOCK)[None, :]
    ov = ((tok[:, :, None] // SLC_BLOCK) == np.arange(n_blk)[None, None, :]).sum(1)
    ov = ov.astype(np.float32) / np.float32(CMP_BLOCK)
    out = np.zeros((N_KV, LANES, n_rows), np.float32)
    for g in range(N_KV):
        other = (1 - g) * HEAD_DIM
        out[g, other:other + n_blk, :n_cmp] = ov.T
    return out


def _attn_kernel(q_ref, kv_ref, kcmp_ref, vcmp_ref, gate_ref, ovt_ref, o_ref,
                 ksel_ref, kwin_ref, vselt_ref, vwint_ref, vcmpt_ref, qsel_ref, acc_ref,
                 sc_ref, pr_ref, *, seq):
    tq = ATT_Q
    pw = 2 * tq
    kc = ATT_PAST
    n_blk = seq // SLC_BLOCK
    w_len = WINDOW + tq
    t = pl.program_id(1)
    t0 = pl.multiple_of(t * tq, tq)
    sub = lax.broadcasted_iota(jnp.int32, (LANES, pw), 0)
    tok_off = lax.broadcasted_iota(jnp.int32, (LANES, pw), 1) & (tq - 1)
    causal = sub <= tok_off
    zero_bf = jnp.zeros((), BF16)

    @pl.when(t == 0)
    def _build_keys():
        key_blk = lax.broadcasted_iota(jnp.int32, (seq, LANES), 0) // SLC_BLOCK
        lane_s = lax.broadcasted_iota(jnp.int32, (seq, LANES), 1)
        lane_w = lax.broadcasted_iota(jnp.int32, (WINDOW, LANES), 1)
        vs_t = kv_ref[0, :, KV_W:2 * KV_W].astype(F32).T
        vw_t = kv_ref[0, :, 3 * KV_W:4 * KV_W].astype(F32).T
        vc_t = vcmp_ref[0].astype(F32).T
        for g in range(N_KV):
            other = (1 - g) * HEAD_DIM
            mine = (lane_s // HEAD_DIM) == g
            onehot = jnp.where(lane_s - other == key_blk, 1.0, 0.0).astype(BF16)
            ksel_ref[g] = jnp.where(mine, kv_ref[0, :, 0:KV_W], onehot)
            kwin_ref[g, WINDOW:, :] = jnp.where(mine, kv_ref[0, :, 2 * KV_W:3 * KV_W], zero_bf)
            kwin_ref[g, 0:WINDOW, :] = jnp.where(lane_w == other, 1.0, 0.0).astype(BF16)

            def with_ones(v_t):
                row = lax.broadcasted_iota(jnp.int32, v_t.shape, 0)
                return jnp.where(row // HEAD_DIM == g, v_t, jnp.where(row == other, 1.0, 0.0)).astype(BF16)

            vselt_ref[g] = with_ones(vs_t)
            vwint_ref[g, :, 0:WINDOW] = with_ones(jnp.zeros((LANES, WINDOW), F32))
            vwint_ref[g, :, WINDOW:] = with_ones(vw_t)
            vcmpt_ref[g] = with_ones(vc_t)

    gates_t = gate_ref[...].T

    def gate_row(branch, h0):
        r = branch * N_HEADS + h0
        return jnp.concatenate([gates_t[r:r + 1], gates_t[r + 1:r + 2]], axis=1)

    pairs = range(N_HEADS // 2)
    group_of = [pi // (GROUP // 2) for pi in pairs]
    other_of = [(1 - g) * HEAD_DIM for g in range(N_KV)]
    mine_rows = [(sub // HEAD_DIM) == g for g in range(N_KV)]

    def normalised(o_t, pi):
        g = group_of[pi]
        return o_t[g * HEAD_DIM:(g + 1) * HEAD_DIM] / o_t[other_of[g]:other_of[g] + 1]


    qts, q_plain, q_mark = [], [], []
    for pi in pairs:
        g = group_of[pi]
        raw_t = q_ref[0, :, pi * LANES:(pi + 1) * LANES].astype(F32).T
        swap_t = jnp.concatenate([raw_t[HEAD_DIM:], raw_t[:HEAD_DIM]], axis=0)
        even_t, odd_t = (raw_t, swap_t) if g == 0 else (swap_t, raw_t)
        q_t = jnp.concatenate([even_t, odd_t], axis=1)
        qts.append(q_t)
        q_plain.append(jnp.where(mine_rows[g], q_t, 0.0).astype(BF16))
        marker = jnp.where(sub == other_of[g], NEG_INF, 0.0)
        q_mark.append(jnp.where(mine_rows[g], q_t, marker).astype(BF16))

    s_cmp = [_dot(kcmp_ref[0], q_plain[pi]) for pi in pairs]
    s_win = [_dot(kwin_ref[group_of[pi], pl.ds(t0, w_len), :], q_mark[pi]) for pi in pairs]
    s_dg = [_dot(ksel_ref[group_of[pi], pl.ds(t0, tq), :], q_plain[pi]) for pi in pairs]

    valid = sub * CMP_STRIDE + (CMP_BLOCK - 1) <= t0 + tok_off
    p_cmp = []
    for pi in pairs:
        s = jnp.where(valid, s_cmp[pi], NEG_INF)
        e = jnp.exp2(s - jnp.max(s, axis=0, keepdims=True))
        p_cmp.append(jnp.where(valid, e / jnp.sum(e, axis=0, keepdims=True), 0.0))
    o_cmps = [_dot(vcmpt_ref[group_of[pi]], p_cmp[pi].astype(BF16))[group_of[pi] * HEAD_DIM:
                                                                      (group_of[pi] + 1) * HEAD_DIM]
              for pi in pairs]
    imp_ts = []
    for g in range(N_KV):
        psum = functools.reduce(jnp.add, [p_cmp[pi][:, :tq] + p_cmp[pi][:, tq:]
                                          for pi in pairs if group_of[pi] == g])
        imp_ts.append(jnp.dot(ovt_ref[g], psum, preferred_element_type=F32,
                              precision=lax.Precision.HIGHEST))

    p_win = []
    for pi in pairs:
        sw = s_win[pi]
        parts = [jnp.where(sub > tok_off, sw[0:tq], NEG_INF), sw[tq:WINDOW],
                 jnp.where(causal, sw[WINDOW:], NEG_INF)]
        m = functools.reduce(jnp.maximum, [jnp.max(s, axis=0, keepdims=True) for s in parts])
        p_win.append(jnp.concatenate([jnp.exp2(s - m).astype(BF16) for s in parts], axis=0))
    o_wins = [normalised(_dot(vwint_ref[group_of[pi], :, pl.ds(t0, w_len)], p_win[pi]), pi) for pi in pairs]

    blk_id = lax.broadcasted_iota(jnp.int32, (n_blk, tq), 0)
    blk_f = blk_id.astype(F32)
    cur = (t0 + lax.broadcasted_iota(jnp.int32, (n_blk, tq), 1)) // SLC_BLOCK
    forced = (blk_id == 0) | (blk_id == cur) | (blk_id == cur - 1)
    for g in range(N_KV):
        other = other_of[g]
        imp = imp_ts[g][other:other + n_blk, :]
        imp = jnp.where(blk_id <= cur, imp + jnp.where(forced, FORCE_BONUS, 0.0), NEG_INF)
        chosen = jnp.zeros((n_blk, tq), F32)
        for _ in range(min(SLC_TOPK, n_blk)):
            top = jnp.max(imp, axis=0, keepdims=True)
            first = jnp.min(jnp.where(imp == top, blk_f, float(n_blk)), axis=0, keepdims=True)
            hit = blk_f == first
            chosen = jnp.where(hit, 1.0, chosen)
            imp = jnp.where(hit, TAKEN, imp)
        keep = (chosen > 0.5) & (blk_id < t * (tq // SLC_BLOCK))
        pen = jnp.where(keep, 0.0, NEG_INF)
        pen = jnp.concatenate(
            [jnp.zeros((n, tq), F32) for n in (other,) if n] + [pen]
            + [jnp.zeros((n, tq), F32) for n in (LANES - other - n_blk,) if n], axis=0)
        pen = jnp.concatenate([pen, pen], axis=1)
        for pi in pairs:
            if group_of[pi] == g:
                qsel_ref[pi] = jnp.where(mine_rows[g], qts[pi], pen).astype(BF16)

    m0, p_dg = [], []
    for pi in pairs:
        s_d = jnp.where(causal, s_dg[pi], NEG_INF)
        m_d = jnp.max(s_d, axis=0, keepdims=True)
        m0.append(m_d)
        p_dg.append(jnp.exp2(s_d - m_d).astype(BF16))
    for pi in pairs:
        acc_ref[pi] = _dot(vselt_ref[group_of[pi], :, pl.ds(t0, tq)], p_dg[pi])

    n_past = (t * tq + kc - 1) // kc
    max_chunk = seq // kc - 1

    def chunk_start(c):
        return pl.multiple_of(jnp.clip(c, 0, max_chunk) * kc, kc)

    def scores_into(c, slot):
        k0 = chunk_start(c)
        for pi in pairs:
            sc_ref[slot, pi] = _dot(ksel_ref[group_of[pi], pl.ds(k0, kc), :], qsel_ref[pi])

    def values_from(c, slot, alphas):
        k0 = chunk_start(c)
        for pi in pairs:
            acc_ref[pi] = alphas[pi] * acc_ref[pi] + _dot(vselt_ref[group_of[pi], :, pl.ds(k0, kc)],
                                                          pr_ref[slot, pi])

    def softmax_slot(slot, ms):
        new_m, alphas = [], []
        for pi in pairs:
            s = sc_ref[slot, pi]
            m_new = jnp.maximum(ms[pi], jnp.max(s, axis=0, keepdims=True))
            alphas.append(jnp.exp2(ms[pi] - m_new))
            pr_ref[slot, pi] = jnp.exp2(s - m_new).astype(BF16)
            new_m.append(m_new)
        return tuple(new_m), tuple(alphas)

    def past_body(i, carry):
        ms, alphas = carry
        c0 = 2 * i
        scores_into(c0 + 1, 1)
        values_from(c0 - 1, 1, alphas)
        ms, alphas = softmax_slot(0, ms)
        scores_into(c0 + 2, 0)
        values_from(c0, 0, alphas)
        return softmax_slot(1, ms)

    scores_into(0, 0)
    for pi in pairs:
        pr_ref[1, pi] = jnp.zeros((kc, pw), BF16)
    n_trips = (n_past + 1) // 2
    _, alphas = lax.fori_loop(0, n_trips, past_body,
                              (tuple(m0), tuple(jnp.ones((1, pw), F32) for _ in pairs)))
    values_from(2 * n_trips - 1, 1, alphas)

    head_out = [None] * N_HEADS
    for pi in pairs:
        h0 = 2 * pi
        mixed = (gate_row(0, h0) * o_cmps[pi] + gate_row(1, h0) * normalised(acc_ref[pi], pi)
                 + gate_row(2, h0) * o_wins[pi])
        head_out[h0] = mixed[:, :tq]
        head_out[h0 + 1] = mixed[:, tq:]

    o_ref[0] = jnp.concatenate(head_out, axis=0).T.astype(BF16)


def _attention(q, kvsw, kcmp, vcmp, gates, batch, seq):
    n_rows = seq // CMP_STRIDE
    n_pairs = N_HEADS // 2
    ovt = jnp.asarray(_overlap_matrix(n_rows, seq // SLC_BLOCK))
    tile = lambda b, t: (b, t, 0)
    per_b = lambda b, t: (b, 0, 0)
    return pl.pallas_call(
        functools.partial(_attn_kernel, seq=seq),
        grid=(batch, seq // ATT_Q),
        in_specs=[pl.BlockSpec((1, ATT_Q, NSA_W), tile),
                  pl.BlockSpec((1, seq, 4 * KV_W), per_b),
                  pl.BlockSpec((1, n_rows, KV_W), per_b),
                  pl.BlockSpec((1, n_rows, KV_W), per_b),
                  pl.BlockSpec((ATT_Q, LANES), lambda b, t: (b * (seq // ATT_Q) + t, 0)),
                  pl.BlockSpec((N_KV, LANES, n_rows), lambda b, t: (0, 0, 0))],
        out_specs=pl.BlockSpec((1, ATT_Q, NSA_W), tile),
        out_shape=jax.ShapeDtypeStruct((batch, seq, NSA_W), BF16),
        scratch_shapes=[pltpu.VMEM((N_KV, seq, LANES), BF16),
                        pltpu.VMEM((N_KV, WINDOW + seq, LANES), BF16),
                        pltpu.VMEM((N_KV, LANES, seq), BF16),
                        pltpu.VMEM((N_KV, LANES, WINDOW + seq), BF16),
                        pltpu.VMEM((N_KV, LANES, n_rows), BF16),
                        pltpu.VMEM((n_pairs, LANES, 2 * ATT_Q), BF16),
                        pltpu.VMEM((n_pairs, LANES, 2 * ATT_Q), F32),
                        pltpu.VMEM((2, n_pairs, ATT_PAST, 2 * ATT_Q), F32),
                        pltpu.VMEM((2, n_pairs, ATT_PAST, 2 * ATT_Q), BF16)],
        compiler_params=_cparams(("arbitrary", "arbitrary")),
        name="nsa_attention",
    )(q.reshape(batch, seq, NSA_W), kvsw.reshape(batch, seq, 4 * KV_W), kcmp, vcmp, gates, ovt)


def _even_out_kernel(a_ref, sa_ref, usb_ref, vn_ref, ws_ref, bias_ref, w_ref, g_ref, x_ref, o_ref):
    lane = lax.broadcasted_iota(jnp.int32, (SGU_CHUNK, LANES), 1)
    low = lane < SGU_GROUP_DIM
    ri = lax.broadcasted_iota(jnp.int32, (SGU_CHUNK, SGU_CHUNK), 0)
    ci = lax.broadcasted_iota(jnp.int32, (SGU_CHUNK, SGU_CHUNK), 1)
    tril = ci <= ri
    wmix = [jnp.where(tril, ws_ref[g], 0.0).astype(BF16) for g in range(SGU_GROUPS)]
    zero = jnp.zeros((), BF16)
    left = (a_ref[...].astype(F32) * sa_ref[...].astype(F32)).astype(BF16)
    rights = []
    for c in range(OUT_ROWS // SGU_CHUNK):
        rows = slice(c * SGU_CHUNK, (c + 1) * SGU_CHUNK)
        blocks = []
        for p in range(SGU_W // LANES):
            cols = slice(p * LANES, (p + 1) * LANES)
            vb = vn_ref[rows, cols]
            mixed = (_dot(wmix[2 * p], jnp.where(low, vb, zero))
                     + _dot(wmix[2 * p + 1], jnp.where(low, zero, vb))
                     + bias_ref[:, cols])
            blocks.append((usb_ref[rows, cols].astype(F32) * mixed).astype(BF16))
        rights.append(jnp.concatenate(blocks, axis=1))
    right = jnp.concatenate(rights, axis=0)
    y = _dot(left, w_ref[0:NSA_W, :]) + _dot(right, w_ref[NSA_W:, :])
    o_ref[...] = x_ref[...] + _rms(y, g_ref[...])


def _even_out(a, sa, usb, vn, sgu_w, sgu_b, w_out, g_post, x2):
    t, d = x2.shape
    bias = jnp.repeat(sgu_b.T, SGU_GROUP_DIM, axis=1)
    row = lambda i: (i, 0)
    fixed = lambda i: (0, 0)
    return pl.pallas_call(
        _even_out_kernel,
        grid=(t // OUT_ROWS,),
        in_specs=[pl.BlockSpec((OUT_ROWS, NSA_W), row), pl.BlockSpec((OUT_ROWS, NSA_W), row),
                  pl.BlockSpec((OUT_ROWS, SGU_W), row), pl.BlockSpec((OUT_ROWS, SGU_W), row),
                  pl.BlockSpec((SGU_GROUPS, SGU_CHUNK, SGU_CHUNK), lambda i: (0, 0, 0)),
                  pl.BlockSpec((SGU_CHUNK, SGU_W), fixed),
                  pl.BlockSpec((NSA_W + SGU_W, d), fixed), pl.BlockSpec((1, d), fixed),
                  pl.BlockSpec((OUT_ROWS, d), row)],
        out_specs=pl.BlockSpec((OUT_ROWS, d), row),
        out_shape=jax.ShapeDtypeStruct((t, d), F32),
        compiler_params=_cparams(("arbitrary",)),
        name="even_out",
    )(a, sa, usb, vn, sgu_w, bias, w_out.astype(BF16), g_post.reshape(1, d), x2)


def _odd_kernel(x_ref, gpre_ref, win_ref, dww_ref, dwb_ref, lng_ref, lnb_ref, wout_ref, gpost_ref,
                o_ref, ypad_ref, conv_ref, sz_ref):
    d = x_ref.shape[-1]
    x = x_ref[0]
    h = _rms(x, gpre_ref[...]).astype(BF16)
    n_slices = d // ODD_COLS

    @pl.when(pl.program_id(1) == 0)
    def _():
        ypad_ref[:, 0:CONV_HALO, :] = jnp.zeros((d // LANES, CONV_HALO, LANES), F32)

    def project_glu(cs):
        c0, c1 = cs * ODD_COLS, (cs + 1) * ODD_COLS
        a = _dot(h, win_ref[:, c0:c1])
        gl = _dot(h, win_ref[:, d + c0:d + c1])
        y = a * jax.nn.sigmoid(gl)
        for j in range(ODD_COLS // LANES):
            ypad_ref[c0 // LANES + j, CONV_HALO:, :] = y[:, j * LANES:(j + 1) * LANES]

    def project_gate(cs):
        c0, c1 = cs * ODD_COLS, (cs + 1) * ODD_COLS
        sz_ref[:, c0:c1] = jax.nn.silu(_dot(h, win_ref[:, 2 * d + c0:2 * d + c1])).astype(BF16)

    shift = CONV_HALO - (CONV_K - 1)

    def conv(cs):
        for lt in range(cs * ODD_COLS // LANES, (cs + 1) * ODD_COLS // LANES):
            cols = slice(lt * LANES, (lt + 1) * LANES)
            taps = [jnp.broadcast_to(dww_ref[k:k + 1, cols], (SUBLANES, LANES)) for k in range(CONV_K)]
            bias = jnp.broadcast_to(dwb_ref[:, cols], (SUBLANES, LANES))
            blocks = CONV_RB // SUBLANES
            for base in range(0, ODD_ROWS, CONV_RB):
                parts = [[None] * CONV_CHAINS for _ in range(blocks)]
                for s in range(CONV_K + CONV_RB - SUBLANES):
                    win = ypad_ref[lt, base + shift + s:base + shift + s + SUBLANES, :]
                    for blk in range(blocks):
                        k = s - blk * SUBLANES
                        if 0 <= k < CONV_K:
                            term = taps[k] * win
                            old = parts[blk][k % CONV_CHAINS]
                            parts[blk][k % CONV_CHAINS] = term if old is None else old + term
                for blk in range(blocks):
                    r0 = base + blk * SUBLANES
                    conv_ref[r0:r0 + SUBLANES, cols] = functools.reduce(jnp.add, parts[blk]) + bias

    project_glu(0)
    for cs in range(n_slices):
        if cs + 1 < n_slices:
            project_glu(cs + 1)
        else:
            for gs in range(n_slices):
                project_gate(gs)
        conv(cs)
    ypad_ref[:, 0:CONV_HALO, :] = ypad_ref[:, ODD_ROWS:ODD_ROWS + CONV_HALO, :]

    y = jax.nn.silu(_layer_norm(conv_ref[...], lng_ref[...], lnb_ref[...]))
    out = _dot((y * sz_ref[...].astype(F32)).astype(BF16), wout_ref[...])
    o_ref[0] = x + _rms(out, gpost_ref[...])


def _odd_layer(x3, g_pre, w_in, dw_w, dw_b, ln_g, ln_b, w_out, g_post):
    batch, seq, d = x3.shape
    tile = lambda b, t: (b, t, 0)
    fixed = lambda b, t: (0, 0)
    vec = lambda v: v.reshape(1, d)
    return pl.pallas_call(
        _odd_kernel,
        grid=(batch, seq // ODD_ROWS),
        in_specs=[pl.BlockSpec((1, ODD_ROWS, d), tile), pl.BlockSpec((1, d), fixed),
                  pl.BlockSpec((d, 3 * d), fixed),
                  pl.BlockSpec((CONV_K, d), fixed),
                  pl.BlockSpec((1, d), fixed), pl.BlockSpec((1, d), fixed), pl.BlockSpec((1, d), fixed),
                  pl.BlockSpec((d, d), fixed), pl.BlockSpec((1, d), fixed)],
        out_specs=pl.BlockSpec((1, ODD_ROWS, d), tile),
        out_shape=jax.ShapeDtypeStruct((batch, seq, d), F32),
        scratch_shapes=[pltpu.VMEM((d // LANES, CONV_HALO + ODD_ROWS, LANES), F32),
                        pltpu.VMEM((ODD_ROWS, d), F32), pltpu.VMEM((ODD_ROWS, d), BF16)],
        compiler_params=_cparams(("arbitrary", "arbitrary")),
        name="odd_layer",
    )(x3, vec(g_pre), w_in.astype(BF16), dw_w, vec(dw_b), vec(ln_g), vec(ln_b),
      w_out.astype(BF16), vec(g_post))


def kernel(x, norm_pre, norm_post, even_w_in, even_cmp_k_pe, even_cmp_k_w1, even_cmp_k_w2,
           even_cmp_v_pe, even_cmp_v_w1, even_cmp_v_w2, even_sgu_ln_g, even_sgu_ln_b,
           even_sgu_w, even_sgu_b, even_w_out, odd_w_in, odd_dw_w, odd_dw_b, odd_ln_g,
           odd_ln_b, odd_w_out):
    batch, seq, d = x.shape
    depth = norm_pre.shape[0]
    for i in range(depth):
        li = i // 2
        if i % 2 == 0:
            x2 = x.reshape(batch * seq, d)
            q, kc, vc, kvsw, sa, usb, vn, gates = _even_in(
                x2, norm_pre[i], even_w_in[li], even_sgu_ln_g[li], even_sgu_ln_b[li])
            cw = _compress_weights(even_cmp_k_pe[li], even_cmp_k_w1[li], even_cmp_k_w2[li],
                                   even_cmp_v_pe[li], even_cmp_v_w1[li], even_cmp_v_w2[li])
            kcmp, vcmp = _compress(kc, vc, batch, seq, cw)
            a = _attention(q, kvsw, kcmp, vcmp, gates, batch, seq)
            x = _even_out(a.reshape(batch * seq, NSA_W), sa, usb, vn, even_sgu_w[li], even_sgu_b[li],
                          even_w_out[li], norm_post[i], x2).reshape(batch, seq, d)
        else:
            x = _odd_layer(x, norm_pre[i], odd_w_in[li], odd_dw_w[li], odd_dw_b[li],
                           odd_ln_g[li], odd_ln_b[li], odd_w_out[li], norm_post[i])
    return x
```

```python
import functools

import jax
import jax.numpy as jnp
import numpy as np
from jax import lax
from jax.experimental import pallas as pl
from jax.experimental.pallas import tpu as pltpu

N_HEADS = 8
N_KV = 2
GROUP = N_HEADS // N_KV
HEAD_DIM = 64
NSA_W = N_HEADS * HEAD_DIM
KV_W = N_KV * HEAD_DIM
CMP_BLOCK = 32
CMP_STRIDE = 16
SLC_BLOCK = 64
SLC_TOPK = 8
WINDOW = 512
FORCE_BONUS = 1e4
NEG_INF = -1e30
TAKEN = -3e38
LOG2E = 1.4426950408889634
SGU_GROUPS = 8
SGU_GROUP_DIM = 64
SGU_W = SGU_GROUPS * SGU_GROUP_DIM
SGU_CHUNK = 128
CONV_K = 31
RMS_EPS = 1e-6
LN_EPS = 1e-5

LANES = 128
SUBLANES = 8
VMEM_LIMIT_BYTES = 56 * 1024 * 1024

IN_ROWS = 512
ATT_Q = 128
ATT_PAST = 256
OUT_ROWS = 512
ODD_ROWS = 512
ODD_COLS = 256
CONV_HALO = 32
CONV_RB = 64
CONV_CHAINS = 1

F32 = jnp.float32
BF16 = jnp.bfloat16


def _cparams(sem):
    return pltpu.CompilerParams(dimension_semantics=sem, vmem_limit_bytes=VMEM_LIMIT_BYTES)


def _rms(x, g):
    return x * lax.rsqrt(jnp.mean(x * x, axis=-1, keepdims=True) + RMS_EPS) * g


def _layer_norm(x, g, b):
    mu = jnp.mean(x, axis=-1, keepdims=True)
    xc = x - mu
    return xc * lax.rsqrt(jnp.mean(xc * xc, axis=-1, keepdims=True) + LN_EPS) * g + b


def _dot(a, b):
    return jnp.dot(a, b, preferred_element_type=F32)


EV_Q, EV_KC, EV_KS, EV_ZA, EV_U, EV_V, EV_ZB, EV_G, EV_END = 0, 512, 768, 1280, 1792, 2304, 2816, 3328, 3456


def _even_in_kernel(x_ref, g_ref, w_ref, lng_ref, lnb_ref,
                    q_ref, kc_ref, vc_ref, kvsw_ref, sa_ref, usb_ref, vn_ref, gate_ref):
    h = _rms(x_ref[...], g_ref[...]).astype(BF16)
    q_ref[...] = (_dot(h, w_ref[:, EV_Q:EV_KC]) * (HEAD_DIM ** -0.5 * LOG2E)).astype(BF16)
    kcvc = _dot(h, w_ref[:, EV_KC:EV_KS])
    kc_ref[...] = kcvc[:, :KV_W]
    vc_ref[...] = kcvc[:, KV_W:]
    kvsw_ref[...] = _dot(h, w_ref[:, EV_KS:EV_ZA]).astype(BF16)
    sa_ref[...] = jax.nn.silu(_dot(h, w_ref[:, EV_ZA:EV_U])).astype(BF16)
    u = _dot(h, w_ref[:, EV_U:EV_V])
    zb = _dot(h, w_ref[:, EV_ZB:EV_G])
    usb_ref[...] = (jax.nn.gelu(u) * jax.nn.silu(zb)).astype(BF16)
    v = jax.nn.gelu(_dot(h, w_ref[:, EV_V:EV_ZB]))
    vn_ref[...] = _layer_norm(v, lng_ref[...], lnb_ref[...]).astype(BF16)
    gate_ref[...] = jax.nn.sigmoid(_dot(h, w_ref[:, EV_G:EV_END]))


def _even_in(x2, g_pre, w_in, ln_g, ln_b):
    t, d = x2.shape
    wb = w_in.astype(BF16)
    w = jnp.concatenate(
        [wb[:, :1280], wb[:, 1304:], wb[:, 1280:1304], jnp.zeros((d, LANES - 3 * N_HEADS), BF16)], axis=1)
    row = lambda i: (i, 0)
    fixed = lambda i: (0, 0)
    outs = [(NSA_W, BF16), (KV_W, F32), (KV_W, F32), (4 * KV_W, BF16), (NSA_W, BF16),
            (SGU_W, BF16), (SGU_W, BF16), (LANES, F32)]
    return pl.pallas_call(
        _even_in_kernel,
        grid=(t // IN_ROWS,),
        in_specs=[pl.BlockSpec((IN_ROWS, d), row), pl.BlockSpec((1, d), fixed),
                  pl.BlockSpec((d, EV_END), fixed), pl.BlockSpec((1, SGU_W), fixed),
                  pl.BlockSpec((1, SGU_W), fixed)],
        out_specs=[pl.BlockSpec((IN_ROWS, n), row) for n, _ in outs],
        out_shape=[jax.ShapeDtypeStruct((t, n), dt) for n, dt in outs],
        compiler_params=_cparams(("arbitrary",)),
        name="even_in",
    )(x2, g_pre.reshape(1, d), w, ln_g.reshape(1, SGU_W), ln_b.reshape(1, SGU_W))


def _compress_kernel(kc_ref, vc_ref, pe_ref, w1_ref, w2_ref, kcmp_ref, vcmp_ref):
    n_rows = kcmp_ref.shape[1]
    rows = lax.broadcasted_iota(jnp.int32, (n_rows, KV_W), 0)
    for ti, (src_ref, dst_ref) in enumerate(((kc_ref, kcmp_ref), (vc_ref, vcmp_ref))):
        kr = jnp.concatenate([src_ref[0, pl.ds(l, n_rows, stride=CMP_STRIDE), :]
                              for l in range(CMP_STRIDE)], axis=1)
        first = _dot((kr + pe_ref[ti, 0]).astype(BF16), w1_ref[ti, 0])
        second = _dot((kr + pe_ref[ti, 1]).astype(BF16), w1_ref[ti, 1])
        pre = first + pltpu.roll(second, n_rows - 1, 0)
        out = _dot(jax.nn.silu(pre).astype(BF16), w2_ref[ti])
        dst_ref[0] = jnp.where(rows < n_rows - 1, out, 0.0).astype(BF16)


def _compress_weights(k_pe, k_w1, k_w2, v_pe, v_w1, v_w2):
    dh, half = HEAD_DIM, CMP_STRIDE
    eye_g = jnp.eye(N_KV, dtype=F32)

    def first_layer(w1, lo):
        w = w1.reshape(CMP_BLOCK, dh, dh)[lo:lo + half]
        return jnp.einsum('lde,gh->lgdhe', w, eye_g).reshape(half * KV_W, KV_W)

    def pe_row(pe, lo):
        return jnp.tile(pe[lo:lo + half], (1, N_KV)).reshape(1, half * KV_W)

    pe = jnp.stack([jnp.stack([pe_row(p, 0), pe_row(p, half)]) for p in (k_pe, v_pe)])
    w1 = jnp.stack([jnp.stack([first_layer(w, 0), first_layer(w, half)]) for w in (k_w1, v_w1)])
    w2 = jnp.stack([jnp.einsum('de,gh->gdhe', w, eye_g).reshape(KV_W, KV_W) for w in (k_w2, v_w2)])
    return pe, w1.astype(BF16), w2.astype(BF16)


def _compress(kc, vc, batch, seq, cw):
    pe, w1, w2 = cw
    n_rows = seq // CMP_STRIDE
    width = CMP_STRIDE * KV_W
    per_b = lambda b: (b, 0, 0)
    return pl.pallas_call(
        _compress_kernel,
        grid=(batch,),
        in_specs=[pl.BlockSpec((1, seq, KV_W), per_b), pl.BlockSpec((1, seq, KV_W), per_b),
                  pl.BlockSpec((2, 2, 1, width), lambda b: (0, 0, 0, 0)),
                  pl.BlockSpec((2, 2, width, KV_W), lambda b: (0, 0, 0, 0)),
                  pl.BlockSpec((2, KV_W, KV_W), lambda b: (0, 0, 0))],
        out_specs=[pl.BlockSpec((1, n_rows, KV_W), per_b)] * 2,
        out_shape=[jax.ShapeDtypeStruct((batch, n_rows, KV_W), BF16)] * 2,
        compiler_params=_cparams(("arbitrary",)),
        name="compress",
    )(kc.reshape(batch, seq, KV_W), vc.reshape(batch, seq, KV_W), pe, w1, w2)


def _overlap_matrix(n_rows, n_blk):
    n_cmp = n_rows - 1
    tok = np.arange(n_cmp)[:, None] * CMP_STRIDE + np.arange(CMP_BLOCK)[None, :]
    ov = ((tok[:, :, None] // SLC_BLOCK) == np.arange(n_blk)[None, None, :]).sum(1)
    ov = ov.astype(np.float32) / np.float32(CMP_BLOCK)
    out = np.zeros((N_KV, LANES, n_rows), np.float32)
    for g in range(N_KV):
        other = (1 - g) * HEAD_DIM
        out[g, other:other + n_blk, :n_cmp] = ov.T
    return out


def _attn_kernel(q_ref, kv_ref, kcmp_ref, vcmp_ref, gate_ref, ovt_ref, o_ref,
                 ksel_ref, kwin_ref, vselt_ref, vwint_ref, vcmpt_ref, qsel_ref, acc_ref,
                 sc_ref, pr_ref, *, seq):
    tq = ATT_Q
    pw = 2 * tq
    kc = ATT_PAST
    n_blk = seq // SLC_BLOCK
    w_len = WINDOW + tq
    t = pl.program_id(1)
    t0 = pl.multiple_of(t * tq, tq)
    sub = lax.broadcasted_iota(jnp.int32, (LANES, pw), 0)
    tok_off = lax.broadcasted_iota(jnp.int32, (LANES, pw), 1) & (tq - 1)
    causal = sub <= tok_off
    zero_bf = jnp.zeros((), BF16)

    @pl.when(t == 0)
    def _build_keys():
        key_blk = lax.broadcasted_iota(jnp.int32, (seq, LANES), 0) // SLC_BLOCK
        lane_s = lax.broadcasted_iota(jnp.int32, (seq, LANES), 1)
        lane_w = lax.broadcasted_iota(jnp.int32, (WINDOW, LANES), 1)
        vs_t = kv_ref[0, :, KV_W:2 * KV_W].astype(F32).T
        vw_t = kv_ref[0, :, 3 * KV_W:4 * KV_W].astype(F32).T
        vc_t = vcmp_ref[0].astype(F32).T
        for g in range(N_KV):
            other = (1 - g) * HEAD_DIM
            mine = (lane_s // HEAD_DIM) == g
            onehot = jnp.where(lane_s - other == key_blk, 1.0, 0.0).astype(BF16)
            ksel_ref[g] = jnp.where(mine, kv_ref[0, :, 0:KV_W], onehot)
            kwin_ref[g, WINDOW:, :] = jnp.where(mine, kv_ref[0, :, 2 * KV_W:3 * KV_W], zero_bf)
            kwin_ref[g, 0:WINDOW, :] = jnp.where(lane_w == other, 1.0, 0.0).astype(BF16)

            def with_ones(v_t):
                row = lax.broadcasted_iota(jnp.int32, v_t.shape, 0)
                return jnp.where(row // HEAD_DIM == g, v_t, jnp.where(row == other, 1.0, 0.0)).astype(BF16)

            vselt_ref[g] = with_ones(vs_t)
            vwint_ref[g, :, 0:WINDOW] = with_ones(jnp.zeros((LANES, WINDOW), F32))
            vwint_ref[g, :, WINDOW:] = with_ones(vw_t)
            vcmpt_ref[g] = with_ones(vc_t)

    gates_t = gate_ref[...].T

    def gate_row(branch, h0):
        r = branch * N_HEADS + h0
        return jnp.concatenate([gates_t[r:r + 1], gates_t[r + 1:r + 2]], axis=1)

    pairs = range(N_HEADS // 2)
    group_of = [pi // (GROUP // 2) for pi in pairs]
    other_of = [(1 - g) * HEAD_DIM for g in range(N_KV)]
    mine_rows = [(sub // HEAD_DIM) == g for g in range(N_KV)]

    def normalised(o_t, pi):
        g = group_of[pi]
        return o_t[g * HEAD_DIM:(g + 1) * HEAD_DIM] / o_t[other_of[g]:other_of[g] + 1]


    qts, q_plain, q_mark = [], [], []
    for pi in pairs:
        g = group_of[pi]
        raw_t = q_ref[0, :, pi * LANES:(pi + 1) * LANES].astype(F32).T
        swap_t = jnp.concatenate([raw_t[HEAD_DIM:], raw_t[:HEAD_DIM]], axis=0)
        even_t, odd_t = (raw_t, swap_t) if g == 0 else (swap_t, raw_t)
        q_t = jnp.concatenate([even_t, odd_t], axis=1)
        qts.append(q_t)
        q_plain.append(jnp.where(mine_rows[g], q_t, 0.0).astype(BF16))
        marker = jnp.where(sub == other_of[g], NEG_INF, 0.0)
        q_mark.append(jnp.where(mine_rows[g], q_t, marker).astype(BF16))

    s_cmp = [_dot(kcmp_ref[0], q_plain[pi]) for pi in pairs]
    s_win = [_dot(kwin_ref[group_of[pi], pl.ds(t0, w_len), :], q_mark[pi]) for pi in pairs]
    s_dg = [_dot(ksel_ref[group_of[pi], pl.ds(t0, tq), :], q_plain[pi]) for pi in pairs]

    valid = sub * CMP_STRIDE + (CMP_BLOCK - 1) <= t0 + tok_off
    p_cmp = []
    for pi in pairs:
        s = jnp.where(valid, s_cmp[pi], NEG_INF)
        e = jnp.exp2(s - jnp.max(s, axis=0, keepdims=True))
        p_cmp.append(jnp.where(valid, e / jnp.sum(e, axis=0, keepdims=True), 0.0))
    o_cmps = [_dot(vcmpt_ref[group_of[pi]], p_cmp[pi].astype(BF16))[group_of[pi] * HEAD_DIM:
                                                                      (group_of[pi] + 1) * HEAD_DIM]
              for pi in pairs]
    imp_ts = []
    for g in range(N_KV):
        psum = functools.reduce(jnp.add, [p_cmp[pi][:, :tq] + p_cmp[pi][:, tq:]
                                          for pi in pairs if group_of[pi] == g])
        imp_ts.append(jnp.dot(ovt_ref[g], psum, preferred_element_type=F32,
                              precision=lax.Precision.HIGHEST))

    p_win = []
    for pi in pairs:
        sw = s_win[pi]
        parts = [jnp.where(sub > tok_off, sw[0:tq], NEG_INF), sw[tq:WINDOW],
                 jnp.where(causal, sw[WINDOW:], NEG_INF)]
        m = functools.reduce(jnp.maximum, [jnp.max(s, axis=0, keepdims=True) for s in parts])
        p_win.append(jnp.concatenate([jnp.exp2(s - m).astype(BF16) for s in parts], axis=0))
    o_wins = [normalised(_dot(vwint_ref[group_of[pi], :, pl.ds(t0, w_len)], p_win[pi]), pi) for pi in pairs]

    blk_id = lax.broadcasted_iota(jnp.int32, (n_blk, tq), 0)
    blk_f = blk_id.astype(F32)
    cur = (t0 + lax.broadcasted_iota(jnp.int32, (n_blk, tq), 1)) // SLC_BLOCK
    forced = (blk_id == 0) | (blk_id == cur) | (blk_id == cur - 1)
    for g in range(N_KV):
        other = other_of[g]
        imp = imp_ts[g][other:other + n_blk, :]
        imp = jnp.where(blk_id <= cur, imp + jnp.where(forced, FORCE_BONUS, 0.0), NEG_INF)
        chosen = jnp.zeros((n_blk, tq), F32)
        for _ in range(min(SLC_TOPK, n_blk)):
            top = jnp.max(imp, axis=0, keepdims=True)
            first = jnp.min(jnp.where(imp == top, blk_f, float(n_blk)), axis=0, keepdims=True)
            hit = blk_f == first
            chosen = jnp.where(hit, 1.0, chosen)
            imp = jnp.where(hit, TAKEN, imp)
        keep = (chosen > 0.5) & (blk_id < t * (tq // SLC_BLOCK))
        pen = jnp.where(keep, 0.0, NEG_INF)
        pen = jnp.concatenate(
            [jnp.zeros((n, tq), F32) for n in (other,) if n] + [pen]
            + [jnp.zeros((n, tq), F32) for n in (LANES - other - n_blk,) if n], axis=0)
        pen = jnp.concatenate([pen, pen], axis=1)
        for pi in pairs:
            if group_of[pi] == g:
                qsel_ref[pi] = jnp.where(mine_rows[g], qts[pi], pen).astype(BF16)

    m0, p_dg = [], []
    for pi in pairs:
        s_d = jnp.where(causal, s_dg[pi], NEG_INF)
        m_d = jnp.max(s_d, axis=0, keepdims=True)
        m0.append(m_d)
        p_dg.append(jnp.exp2(s_d - m_d).astype(BF16))
    for pi in pairs:
        acc_ref[pi] = _dot(vselt_ref[group_of[pi], :, pl.ds(t0, tq)], p_dg[pi])

    n_past = (t * tq + kc - 1) // kc
    max_chunk = seq // kc - 1

    def chunk_start(c):
        return pl.multiple_of(jnp.clip(c, 0, max_chunk) * kc, kc)

    def scores_into(c, slot, only=None):
        k0 = chunk_start(c)
        for pi in (pairs if only is None else (only,)):
            sc_ref[slot, pi] = _dot(ksel_ref[group_of[pi], pl.ds(k0, kc), :], qsel_ref[pi])

    def values_from(c, slot, alphas, only=None):
        k0 = chunk_start(c)
        for pi in (pairs if only is None else (only,)):
            acc_ref[pi] = alphas[pi] * acc_ref[pi] + _dot(vselt_ref[group_of[pi], :, pl.ds(k0, kc)],
                                                          pr_ref[slot, pi])

    def softmax_pair(slot, pi, m_old):
        s = sc_ref[slot, pi]
        m_new = jnp.maximum(m_old, jnp.max(s, axis=0, keepdims=True))
        pr_ref[slot, pi] = jnp.exp2(s - m_new).astype(BF16)
        return m_new, jnp.exp2(m_old - m_new)

    def half_trip(c, slot, ms, alphas):
        new_m, new_a = [], []
        for pi in pairs:
            scores_into(c + 1, 1 - slot, only=pi)
            values_from(c - 1, 1 - slot, alphas, only=pi)
            m_new, alpha = softmax_pair(slot, pi, ms[pi])
            new_m.append(m_new)
            new_a.append(alpha)
        return tuple(new_m), tuple(new_a)

    def past_body(i, carry):
        ms, alphas = carry
        ms, alphas = half_trip(2 * i, 0, ms, alphas)
        return half_trip(2 * i + 1, 1, ms, alphas)

    scores_into(0, 0)
    for pi in pairs:
        pr_ref[1, pi] = jnp.zeros((kc, pw), BF16)
    n_trips = (n_past + 1) // 2
    _, alphas = lax.fori_loop(0, n_trips, past_body,
                              (tuple(m0), tuple(jnp.ones((1, pw), F32) for _ in pairs)))
    values_from(2 * n_trips - 1, 1, alphas)

    head_out = [None] * N_HEADS
    for pi in pairs:
        h0 = 2 * pi
        mixed = (gate_row(0, h0) * o_cmps[pi] + gate_row(1, h0) * normalised(acc_ref[pi], pi)
                 + gate_row(2, h0) * o_wins[pi])
        head_out[h0] = mixed[:, :tq]
        head_out[h0 + 1] = mixed[:, tq:]

    o_ref[0] = jnp.concatenate(head_out, axis=0).T.astype(BF16)


def _attention(q, kvsw, kcmp, vcmp, gates, batch, seq):
    n_rows = seq // CMP_STRIDE
    n_pairs = N_HEADS // 2
    ovt = jnp.asarray(_overlap_matrix(n_rows, seq // SLC_BLOCK))
    tile = lambda b, t: (b, t, 0)
    per_b = lambda b, t: (b, 0, 0)
    return pl.pallas_call(
        functools.partial(_attn_kernel, seq=seq),
        grid=(batch, seq // ATT_Q),
        in_specs=[pl.BlockSpec((1, ATT_Q, NSA_W), tile),
                  pl.BlockSpec((1, seq, 4 * KV_W), per_b),
                  pl.BlockSpec((1, n_rows, KV_W), per_b),
                  pl.BlockSpec((1, n_rows, KV_W), per_b),
                  pl.BlockSpec((ATT_Q, LANES), lambda b, t: (b * (seq // ATT_Q) + t, 0)),
                  pl.BlockSpec((N_KV, LANES, n_rows), lambda b, t: (0, 0, 0))],
        out_specs=pl.BlockSpec((1, ATT_Q, NSA_W), tile),
        out_shape=jax.ShapeDtypeStruct((batch, seq, NSA_W), BF16),
        scratch_shapes=[pltpu.VMEM((N_KV, seq, LANES), BF16),
                        pltpu.VMEM((N_KV, WINDOW + seq, LANES), BF16),
                        pltpu.VMEM((N_KV, LANES, seq), BF16),
                        pltpu.VMEM((N_KV, LANES, WINDOW + seq), BF16),
                        pltpu.VMEM((N_KV, LANES, n_rows), BF16),
                        pltpu.VMEM((n_pairs, LANES, 2 * ATT_Q), BF16),
                        pltpu.VMEM((n_pairs, LANES, 2 * ATT_Q), F32),
                        pltpu.VMEM((2, n_pairs, ATT_PAST, 2 * ATT_Q), F32),
                        pltpu.VMEM((2, n_pairs, ATT_PAST, 2 * ATT_Q), BF16)],
        compiler_params=_cparams(("arbitrary", "arbitrary")),
        name="nsa_attention",
    )(q.reshape(batch, seq, NSA_W), kvsw.reshape(batch, seq, 4 * KV_W), kcmp, vcmp, gates, ovt)


def _even_out_kernel(a_ref, sa_ref, usb_ref, vn_ref, ws_ref, bias_ref, w_ref, g_ref, x_ref, o_ref):
    lane = lax.broadcasted_iota(jnp.int32, (SGU_CHUNK, LANES), 1)
    low = lane < SGU_GROUP_DIM
    ri = lax.broadcasted_iota(jnp.int32, (SGU_CHUNK, SGU_CHUNK), 0)
    ci = lax.broadcasted_iota(jnp.int32, (SGU_CHUNK, SGU_CHUNK), 1)
    tril = ci <= ri
    wmix = [jnp.where(tril, ws_ref[g], 0.0).astype(BF16) for g in range(SGU_GROUPS)]
    zero = jnp.zeros((), BF16)
    left = (a_ref[...].astype(F32) * sa_ref[...].astype(F32)).astype(BF16)
    rights = []
    for c in range(OUT_ROWS // SGU_CHUNK):
        rows = slice(c * SGU_CHUNK, (c + 1) * SGU_CHUNK)
        blocks = []
        for p in range(SGU_W // LANES):
            cols = slice(p * LANES, (p + 1) * LANES)
            vb = vn_ref[rows, cols]
            mixed = (_dot(wmix[2 * p], jnp.where(low, vb, zero))
                     + _dot(wmix[2 * p + 1], jnp.where(low, zero, vb))
                     + bias_ref[:, cols])
            blocks.append((usb_ref[rows, cols].astype(F32) * mixed).astype(BF16))
        rights.append(jnp.concatenate(blocks, axis=1))
    right = jnp.concatenate(rights, axis=0)
    y = _dot(left, w_ref[0:NSA_W, :]) + _dot(right, w_ref[NSA_W:, :])
    o_ref[...] = x_ref[...] + _rms(y, g_ref[...])


def _even_out(a, sa, usb, vn, sgu_w, sgu_b, w_out, g_post, x2):
    t, d = x2.shape
    bias = jnp.repeat(sgu_b.T, SGU_GROUP_DIM, axis=1)
    row = lambda i: (i, 0)
    fixed = lambda i: (0, 0)
    return pl.pallas_call(
        _even_out_kernel,
        grid=(t // OUT_ROWS,),
        in_specs=[pl.BlockSpec((OUT_ROWS, NSA_W), row), pl.BlockSpec((OUT_ROWS, NSA_W), row),
                  pl.BlockSpec((OUT_ROWS, SGU_W), row), pl.BlockSpec((OUT_ROWS, SGU_W), row),
                  pl.BlockSpec((SGU_GROUPS, SGU_CHUNK, SGU_CHUNK), lambda i: (0, 0, 0)),
                  pl.BlockSpec((SGU_CHUNK, SGU_W), fixed),
                  pl.BlockSpec((NSA_W + SGU_W, d), fixed), pl.BlockSpec((1, d), fixed),
                  pl.BlockSpec((OUT_ROWS, d), row)],
        out_specs=pl.BlockSpec((OUT_ROWS, d), row),
        out_shape=jax.ShapeDtypeStruct((t, d), F32),
        compiler_params=_cparams(("arbitrary",)),
        name="even_out",
    )(a, sa, usb, vn, sgu_w, bias, w_out.astype(BF16), g_post.reshape(1, d), x2)


def _odd_kernel(x_ref, gpre_ref, win_ref, dww_ref, dwb_ref, lng_ref, lnb_ref, wout_ref, gpost_ref,
                o_ref, ypad_ref, conv_ref, sz_ref):
    d = x_ref.shape[-1]
    x = x_ref[0]
    h = _rms(x, gpre_ref[...]).astype(BF16)
    n_slices = d // ODD_COLS

    @pl.when(pl.program_id(1) == 0)
    def _():
        ypad_ref[:, 0:CONV_HALO, :] = jnp.zeros((d // LANES, CONV_HALO, LANES), F32)

    def project_glu(cs):
        c0, c1 = cs * ODD_COLS, (cs + 1) * ODD_COLS
        a = _dot(h, win_ref[:, c0:c1])
        gl = _dot(h, win_ref[:, d + c0:d + c1])
        y = a * jax.nn.sigmoid(gl)
        for j in range(ODD_COLS // LANES):
            ypad_ref[c0 // LANES + j, CONV_HALO:, :] = y[:, j * LANES:(j + 1) * LANES]

    def project_gate(cs):
        c0, c1 = cs * ODD_COLS, (cs + 1) * ODD_COLS
        sz_ref[:, c0:c1] = jax.nn.silu(_dot(h, win_ref[:, 2 * d + c0:2 * d + c1])).astype(BF16)

    shift = CONV_HALO - (CONV_K - 1)

    def conv(cs):
        for lt in range(cs * ODD_COLS // LANES, (cs + 1) * ODD_COLS // LANES):
            cols = slice(lt * LANES, (lt + 1) * LANES)
            taps = [jnp.broadcast_to(dww_ref[k:k + 1, cols], (SUBLANES, LANES)) for k in range(CONV_K)]
            bias = jnp.broadcast_to(dwb_ref[:, cols], (SUBLANES, LANES))
            blocks = CONV_RB // SUBLANES
            for base in range(0, ODD_ROWS, CONV_RB):
                parts = [[None] * CONV_CHAINS for _ in range(blocks)]
                for s in range(CONV_K + CONV_RB - SUBLANES):
                    win = ypad_ref[lt, base + shift + s:base + shift + s + SUBLANES, :]
                    for blk in range(blocks):
                        k = s - blk * SUBLANES
                        if 0 <= k < CONV_K:
                            term = taps[k] * win
                            old = parts[blk][k % CONV_CHAINS]
                            parts[blk][k % CONV_CHAINS] = term if old is None else old + term
                for blk in range(blocks):
                    r0 = base + blk * SUBLANES
                    conv_ref[r0:r0 + SUBLANES, cols] = functools.reduce(jnp.add, parts[blk]) + bias

    project_glu(0)
    for cs in range(n_slices):
        if cs + 1 < n_slices:
            project_glu(cs + 1)
        else:
            for gs in range(n_slices):
                project_gate(gs)
        conv(cs)
    ypad_ref[:, 0:CONV_HALO, :] = ypad_ref[:, ODD_ROWS:ODD_ROWS + CONV_HALO, :]

    y = jax.nn.silu(_layer_norm(conv_ref[...], lng_ref[...], lnb_ref[...]))
    out = _dot((y * sz_ref[...].astype(F32)).astype(BF16), wout_ref[...])
    o_ref[0] = x + _rms(out, gpost_ref[...])


def _odd_layer(x3, g_pre, w_in, dw_w, dw_b, ln_g, ln_b, w_out, g_post):
    batch, seq, d = x3.shape
    tile = lambda b, t: (b, t, 0)
    fixed = lambda b, t: (0, 0)
    vec = lambda v: v.reshape(1, d)
    return pl.pallas_call(
        _odd_kernel,
        grid=(batch, seq // ODD_ROWS),
        in_specs=[pl.BlockSpec((1, ODD_ROWS, d), tile), pl.BlockSpec((1, d), fixed),
                  pl.BlockSpec((d, 3 * d), fixed),
                  pl.BlockSpec((CONV_K, d), fixed),
                  pl.BlockSpec((1, d), fixed), pl.BlockSpec((1, d), fixed), pl.BlockSpec((1, d), fixed),
                  pl.BlockSpec((d, d), fixed), pl.BlockSpec((1, d), fixed)],
        out_specs=pl.BlockSpec((1, ODD_ROWS, d), tile),
        out_shape=jax.ShapeDtypeStruct((batch, seq, d), F32),
        scratch_shapes=[pltpu.VMEM((d // LANES, CONV_HALO + ODD_ROWS, LANES), F32),
                        pltpu.VMEM((ODD_ROWS, d), F32), pltpu.VMEM((ODD_ROWS, d), BF16)],
        compiler_params=_cparams(("arbitrary", "arbitrary")),
        name="odd_layer",
    )(x3, vec(g_pre), w_in.astype(BF16), dw_w, vec(dw_b), vec(ln_g), vec(ln_b),
      w_out.astype(BF16), vec(g_post))


def kernel(x, norm_pre, norm_post, even_w_in, even_cmp_k_pe, even_cmp_k_w1, even_cmp_k_w2,
           even_cmp_v_pe, even_cmp_v_w1, even_cmp_v_w2, even_sgu_ln_g, even_sgu_ln_b,
           even_sgu_w, even_sgu_b, even_w_out, odd_w_in, odd_dw_w, odd_dw_b, odd_ln_g,
           odd_ln_b, odd_w_out):
    batch, seq, d = x.shape
    depth = norm_pre.shape[0]
    for i in range(depth):
        li = i // 2
        if i % 2 == 0:
            x2 = x.reshape(batch * seq, d)
            q, kc, vc, kvsw, sa, usb, vn, gates = _even_in(
                x2, norm_pre[i], even_w_in[li], even_sgu_ln_g[li], even_sgu_ln_b[li])
            cw = _compress_weights(even_cmp_k_pe[li], even_cmp_k_w1[li], even_cmp_k_w2[li],
                                   even_cmp_v_pe[li], even_cmp_v_w1[li], even_cmp_v_w2[li])
            kcmp, vcmp = _compress(kc, vc, batch, seq, cw)
            a = _attention(q, kvsw, kcmp, vcmp, gates, batch, seq)
            x = _even_out(a.reshape(batch * seq, NSA_W), sa, usb, vn, even_sgu_w[li], even_sgu_b[li],
                          even_w_out[li], norm_post[i], x2).reshape(batch, seq, d)
        else:
            x = _odd_layer(x, norm_pre[i], odd_w_in[li], odd_dw_w[li], odd_dw_b[li],
                           odd_ln_g[li], odd_ln_b[li], odd_w_out[li], norm_post[i])
    return x
```

```python
import functools

import jax
import jax.numpy as jnp
import numpy as np
from jax import lax
from jax.experimental import pallas as pl
from jax.experimental.pallas import tpu as pltpu

N_HEADS = 8
N_KV = 2
GROUP = N_HEADS // N_KV
HEAD_DIM = 64
NSA_W = N_HEADS * HEAD_DIM
KV_W = N_KV * HEAD_DIM
CMP_BLOCK = 32
CMP_STRIDE = 16
SLC_BLOCK = 64
SLC_TOPK = 8
WINDOW = 512
FORCE_BONUS = 1e4
NEG_INF = -1e30
TAKEN = -3e38
LOG2E = 1.4426950408889634
SGU_GROUPS = 8
SGU_GROUP_DIM = 64
SGU_W = SGU_GROUPS * SGU_GROUP_DIM
SGU_CHUNK = 128
CONV_K = 31
RMS_EPS = 1e-6
LN_EPS = 1e-5

LANES = 128
SUBLANES = 8
VMEM_LIMIT_BYTES = 56 * 1024 * 1024

IN_ROWS = 512
ATT_Q = 128
ATT_PAST = 256
OUT_ROWS = 512
EVEN_OUT_BUFFERS = 3
ODD_ROWS = 512
ODD_COLS = 256
CONV_HALO = 32
CONV_RB = 64
CONV_CHAINS = 1

F32 = jnp.float32
BF16 = jnp.bfloat16


def _cparams(sem):
    return pltpu.CompilerParams(dimension_semantics=sem, vmem_limit_bytes=VMEM_LIMIT_BYTES)


def _rms(x, g):
    return x * lax.rsqrt(jnp.mean(x * x, axis=-1, keepdims=True) + RMS_EPS) * g


def _layer_norm(x, g, b):
    mu = jnp.mean(x, axis=-1, keepdims=True)
    xc = x - mu
    return xc * lax.rsqrt(jnp.mean(xc * xc, axis=-1, keepdims=True) + LN_EPS) * g + b


def _dot(a, b):
    return jnp.dot(a, b, preferred_element_type=F32)


EV_Q, EV_KC, EV_KS, EV_ZA, EV_U, EV_V, EV_ZB, EV_G, EV_END = 0, 512, 768, 1280, 1792, 2304, 2816, 3328, 3456


def _even_in_kernel(x_ref, g_ref, w_ref, lng_ref, lnb_ref,
                    q_ref, kc_ref, vc_ref, kvsw_ref, sa_ref, usb_ref, vn_ref, gate_ref):
    h = _rms(x_ref[...], g_ref[...]).astype(BF16)
    q_ref[...] = (_dot(h, w_ref[:, EV_Q:EV_KC]) * (HEAD_DIM ** -0.5 * LOG2E)).astype(BF16)
    kcvc = _dot(h, w_ref[:, EV_KC:EV_KS])
    kc_ref[...] = kcvc[:, :KV_W]
    vc_ref[...] = kcvc[:, KV_W:]
    kvsw_ref[...] = _dot(h, w_ref[:, EV_KS:EV_ZA]).astype(BF16)
    sa_ref[...] = jax.nn.silu(_dot(h, w_ref[:, EV_ZA:EV_U])).astype(BF16)
    u = _dot(h, w_ref[:, EV_U:EV_V])
    zb = _dot(h, w_ref[:, EV_ZB:EV_G])
    usb_ref[...] = (jax.nn.gelu(u) * jax.nn.silu(zb)).astype(BF16)
    v = jax.nn.gelu(_dot(h, w_ref[:, EV_V:EV_ZB]))
    vn_ref[...] = _layer_norm(v, lng_ref[...], lnb_ref[...]).astype(BF16)
    gate_ref[...] = jax.nn.sigmoid(_dot(h, w_ref[:, EV_G:EV_END]))


def _even_in(x2, g_pre, w_in, ln_g, ln_b):
    t, d = x2.shape
    wb = w_in.astype(BF16)
    w = jnp.concatenate(
        [wb[:, :1280], wb[:, 1304:], wb[:, 1280:1304], jnp.zeros((d, LANES - 3 * N_HEADS), BF16)], axis=1)
    row = lambda i: (i, 0)
    fixed = lambda i: (0, 0)
    outs = [(NSA_W, BF16), (KV_W, F32), (KV_W, F32), (4 * KV_W, BF16), (NSA_W, BF16),
            (SGU_W, BF16), (SGU_W, BF16), (LANES, F32)]
    return pl.pallas_call(
        _even_in_kernel,
        grid=(t // IN_ROWS,),
        in_specs=[pl.BlockSpec((IN_ROWS, d), row), pl.BlockSpec((1, d), fixed),
                  pl.BlockSpec((d, EV_END), fixed), pl.BlockSpec((1, SGU_W), fixed),
                  pl.BlockSpec((1, SGU_W), fixed)],
        out_specs=[pl.BlockSpec((IN_ROWS, n), row) for n, _ in outs],
        out_shape=[jax.ShapeDtypeStruct((t, n), dt) for n, dt in outs],
        compiler_params=_cparams(("arbitrary",)),
        name="even_in",
    )(x2, g_pre.reshape(1, d), w, ln_g.reshape(1, SGU_W), ln_b.reshape(1, SGU_W))


def _compress_kernel(kc_ref, vc_ref, pe_ref, w1_ref, w2_ref, kcmp_ref, vcmp_ref):
    n_rows = kcmp_ref.shape[1]
    rows = lax.broadcasted_iota(jnp.int32, (n_rows, KV_W), 0)
    for ti, (src_ref, dst_ref) in enumerate(((kc_ref, kcmp_ref), (vc_ref, vcmp_ref))):
        kr = jnp.concatenate([src_ref[0, pl.ds(l, n_rows, stride=CMP_STRIDE), :]
                              for l in range(CMP_STRIDE)], axis=1)
        first = _dot((kr + pe_ref[ti, 0]).astype(BF16), w1_ref[ti, 0])
        second = _dot((kr + pe_ref[ti, 1]).astype(BF16), w1_ref[ti, 1])
        pre = first + pltpu.roll(second, n_rows - 1, 0)
        out = _dot(jax.nn.silu(pre).astype(BF16), w2_ref[ti])
        dst_ref[0] = jnp.where(rows < n_rows - 1, out, 0.0).astype(BF16)


def _compress_weights(k_pe, k_w1, k_w2, v_pe, v_w1, v_w2):
    dh, half = HEAD_DIM, CMP_STRIDE
    eye_g = jnp.eye(N_KV, dtype=F32)

    def first_layer(w1, lo):
        w = w1.reshape(CMP_BLOCK, dh, dh)[lo:lo + half]
        return jnp.einsum('lde,gh->lgdhe', w, eye_g).reshape(half * KV_W, KV_W)

    def pe_row(pe, lo):
        return jnp.tile(pe[lo:lo + half], (1, N_KV)).reshape(1, half * KV_W)

    pe = jnp.stack([jnp.stack([pe_row(p, 0), pe_row(p, half)]) for p in (k_pe, v_pe)])
    w1 = jnp.stack([jnp.stack([first_layer(w, 0), first_layer(w, half)]) for w in (k_w1, v_w1)])
    w2 = jnp.stack([jnp.einsum('de,gh->gdhe', w, eye_g).reshape(KV_W, KV_W) for w in (k_w2, v_w2)])
    return pe, w1.astype(BF16), w2.astype(BF16)


def _compress(kc, vc, batch, seq, cw):
    pe, w1, w2 = cw
    n_rows = seq // CMP_STRIDE
    width = CMP_STRIDE * KV_W
    per_b = lambda b: (b, 0, 0)
    return pl.pallas_call(
        _compress_kernel,
        grid=(batch,),
        in_specs=[pl.BlockSpec((1, seq, KV_W), per_b), pl.BlockSpec((1, seq, KV_W), per_b),
                  pl.BlockSpec((2, 2, 1, width), lambda b: (0, 0, 0, 0)),
                  pl.BlockSpec((2, 2, width, KV_W), lambda b: (0, 0, 0, 0)),
                  pl.BlockSpec((2, KV_W, KV_W), lambda b: (0, 0, 0))],
        out_specs=[pl.BlockSpec((1, n_rows, KV_W), per_b)] * 2,
        out_shape=[jax.ShapeDtypeStruct((batch, n_rows, KV_W), BF16)] * 2,
        compiler_params=_cparams(("arbitrary",)),
        name="compress",
    )(kc.reshape(batch, seq, KV_W), vc.reshape(batch, seq, KV_W), pe, w1, w2)


def _overlap_matrix(n_rows, n_blk):
    n_cmp = n_rows - 1
    tok = np.arange(n_cmp)[:, None] * CMP_STRIDE + np.arange(CMP_BLOCK)[None, :]
    ov = ((tok[:, :, None] // SLC_BLOCK) == np.arange(n_blk)[None, None, :]).sum(1)
    ov = ov.astype(np.float32) / np.float32(CMP_BLOCK)
    out = np.zeros((N_KV, LANES, n_rows), np.float32)
    for g in range(N_KV):
        other = (1 - g) * HEAD_DIM
        out[g, other:other + n_blk, :n_cmp] = ov.T
    return out


def _attn_kernel(q_ref, kv_ref, kcmp_ref, vcmp_ref, gate_ref, ovt_ref, o_ref,
                 ksel_ref, kwin_ref, vselt_ref, vwint_ref, vcmpt_ref, qsel_ref, acc_ref,
                 sc_ref, pr_ref, *, seq):
    tq = ATT_Q
    pw = 2 * tq
    kc = ATT_PAST
    n_blk = seq // SLC_BLOCK
    w_len = WINDOW + tq
    t = pl.program_id(1)
    t0 = pl.multiple_of(t * tq, tq)
    sub = lax.broadcasted_iota(jnp.int32, (LANES, pw), 0)
    tok_off = lax.broadcasted_iota(jnp.int32, (LANES, pw), 1) & (tq - 1)
    causal = sub <= tok_off
    zero_bf = jnp.zeros((), BF16)

    @pl.when(t == 0)
    def _build_keys():
        key_blk = lax.broadcasted_iota(jnp.int32, (seq, LANES), 0) // SLC_BLOCK
        lane_s = lax.broadcasted_iota(jnp.int32, (seq, LANES), 1)
        lane_w = lax.broadcasted_iota(jnp.int32, (WINDOW, LANES), 1)
        vs_t = kv_ref[0, :, KV_W:2 * KV_W].astype(F32).T
        vw_t = kv_ref[0, :, 3 * KV_W:4 * KV_W].astype(F32).T
        vc_t = vcmp_ref[0].astype(F32).T
        for g in range(N_KV):
            other = (1 - g) * HEAD_DIM
            mine = (lane_s // HEAD_DIM) == g
            onehot = jnp.where(lane_s - other == key_blk, 1.0, 0.0).astype(BF16)
            ksel_ref[g] = jnp.where(mine, kv_ref[0, :, 0:KV_W], onehot)
            kwin_ref[g, WINDOW:, :] = jnp.where(mine, kv_ref[0, :, 2 * KV_W:3 * KV_W], zero_bf)
            kwin_ref[g, 0:WINDOW, :] = jnp.where(lane_w == other, 1.0, 0.0).astype(BF16)

            def with_ones(v_t):
                row = lax.broadcasted_iota(jnp.int32, v_t.shape, 0)
                return jnp.where(row // HEAD_DIM == g, v_t, jnp.where(row == other, 1.0, 0.0)).astype(BF16)

            vselt_ref[g] = with_ones(vs_t)
            vwint_ref[g, :, 0:WINDOW] = with_ones(jnp.zeros((LANES, WINDOW), F32))
            vwint_ref[g, :, WINDOW:] = with_ones(vw_t)
            vcmpt_ref[g] = with_ones(vc_t)

    gates_t = gate_ref[...].T

    def gate_row(branch, h0):
        r = branch * N_HEADS + h0
        return jnp.concatenate([gates_t[r:r + 1], gates_t[r + 1:r + 2]], axis=1)

    pairs = range(N_HEADS // 2)
    group_of = [pi // (GROUP // 2) for pi in pairs]
    other_of = [(1 - g) * HEAD_DIM for g in range(N_KV)]
    mine_rows = [(sub // HEAD_DIM) == g for g in range(N_KV)]

    def normalised(o_t, pi):
        g = group_of[pi]
        return o_t[g * HEAD_DIM:(g + 1) * HEAD_DIM] / o_t[other_of[g]:other_of[g] + 1]


    qts, q_plain, q_mark = [], [], []
    for pi in pairs:
        g = group_of[pi]
        raw_t = q_ref[0, :, pi * LANES:(pi + 1) * LANES].astype(F32).T
        swap_t = jnp.concatenate([raw_t[HEAD_DIM:], raw_t[:HEAD_DIM]], axis=0)
        even_t, odd_t = (raw_t, swap_t) if g == 0 else (swap_t, raw_t)
        q_t = jnp.concatenate([even_t, odd_t], axis=1)
        qts.append(q_t)
        q_plain.append(jnp.where(mine_rows[g], q_t, 0.0).astype(BF16))
        marker = jnp.where(sub == other_of[g], NEG_INF, 0.0)
        q_mark.append(jnp.where(mine_rows[g], q_t, marker).astype(BF16))

    s_cmp = [_dot(kcmp_ref[0], q_plain[pi]) for pi in pairs]
    s_win = [_dot(kwin_ref[group_of[pi], pl.ds(t0, w_len), :], q_mark[pi]) for pi in pairs]
    s_dg = [_dot(ksel_ref[group_of[pi], pl.ds(t0, tq), :], q_plain[pi]) for pi in pairs]

    valid = sub * CMP_STRIDE + (CMP_BLOCK - 1) <= t0 + tok_off
    p_cmp = []
    for pi in pairs:
        s = jnp.where(valid, s_cmp[pi], NEG_INF)
        e = jnp.exp2(s - jnp.max(s, axis=0, keepdims=True))
        p_cmp.append(jnp.where(valid, e / jnp.sum(e, axis=0, keepdims=True), 0.0))
    o_cmps = [_dot(vcmpt_ref[group_of[pi]], p_cmp[pi].astype(BF16))[group_of[pi] * HEAD_DIM:
                                                                      (group_of[pi] + 1) * HEAD_DIM]
              for pi in pairs]
    imp_ts = []
    for g in range(N_KV):
        psum = functools.reduce(jnp.add, [p_cmp[pi][:, :tq] + p_cmp[pi][:, tq:]
                                          for pi in pairs if group_of[pi] == g])
        imp_ts.append(jnp.dot(ovt_ref[g], psum, preferred_element_type=F32,
                              precision=lax.Precision.HIGHEST))

    p_win = []
    for pi in pairs:
        sw = s_win[pi]
        parts = [jnp.where(sub > tok_off, sw[0:tq], NEG_INF), sw[tq:WINDOW],
                 jnp.where(causal, sw[WINDOW:], NEG_INF)]
        m = functools.reduce(jnp.maximum, [jnp.max(s, axis=0, keepdims=True) for s in parts])
        p_win.append(jnp.concatenate([jnp.exp2(s - m).astype(BF16) for s in parts], axis=0))
    o_wins = [normalised(_dot(vwint_ref[group_of[pi], :, pl.ds(t0, w_len)], p_win[pi]), pi) for pi in pairs]

    blk_id = lax.broadcasted_iota(jnp.int32, (n_blk, tq), 0)
    blk_f = blk_id.astype(F32)
    cur = (t0 + lax.broadcasted_iota(jnp.int32, (n_blk, tq), 1)) // SLC_BLOCK
    forced = (blk_id == 0) | (blk_id == cur) | (blk_id == cur - 1)
    for g in range(N_KV):
        other = other_of[g]
        imp = imp_ts[g][other:other + n_blk, :]
        imp = jnp.where(blk_id <= cur, imp + jnp.where(forced, FORCE_BONUS, 0.0), NEG_INF)
        chosen = jnp.zeros((n_blk, tq), F32)
        for _ in range(min(SLC_TOPK, n_blk)):
            top = jnp.max(imp, axis=0, keepdims=True)
            first = jnp.min(jnp.where(imp == top, blk_f, float(n_blk)), axis=0, keepdims=True)
            hit = blk_f == first
            chosen = jnp.where(hit, 1.0, chosen)
            imp = jnp.where(hit, TAKEN, imp)
        keep = (chosen > 0.5) & (blk_id < t * (tq // SLC_BLOCK))
        pen = jnp.where(keep, 0.0, NEG_INF)
        pen = jnp.concatenate(
            [jnp.zeros((n, tq), F32) for n in (other,) if n] + [pen]
            + [jnp.zeros((n, tq), F32) for n in (LANES - other - n_blk,) if n], axis=0)
        pen = jnp.concatenate([pen, pen], axis=1)
        for pi in pairs:
            if group_of[pi] == g:
                qsel_ref[pi] = jnp.where(mine_rows[g], qts[pi], pen).astype(BF16)

    m0, p_dg = [], []
    for pi in pairs:
        s_d = jnp.where(causal, s_dg[pi], NEG_INF)
        m_d = jnp.max(s_d, axis=0, keepdims=True)
        m0.append(m_d)
        p_dg.append(jnp.exp2(s_d - m_d).astype(BF16))
    for pi in pairs:
        acc_ref[pi] = _dot(vselt_ref[group_of[pi], :, pl.ds(t0, tq)], p_dg[pi])

    n_past = (t * tq + kc - 1) // kc
    max_chunk = seq // kc - 1

    def chunk_start(c):
        return pl.multiple_of(jnp.clip(c, 0, max_chunk) * kc, kc)

    def scores_into(c, slot, only=None):
        k0 = chunk_start(c)
        for pi in (pairs if only is None else (only,)):
            sc_ref[slot, pi] = _dot(ksel_ref[group_of[pi], pl.ds(k0, kc), :], qsel_ref[pi])

    def values_from(c, slot, alphas, only=None):
        k0 = chunk_start(c)
        for pi in (pairs if only is None else (only,)):
            acc_ref[pi] = alphas[pi] * acc_ref[pi] + _dot(vselt_ref[group_of[pi], :, pl.ds(k0, kc)],
                                                          pr_ref[slot, pi])

    def softmax_pair(slot, pi, m_old):
        s = sc_ref[slot, pi]
        m_new = jnp.maximum(m_old, jnp.max(s, axis=0, keepdims=True))
        pr_ref[slot, pi] = jnp.exp2(s - m_new).astype(BF16)
        return m_new, jnp.exp2(m_old - m_new)

    def half_trip(c, slot, ms, alphas):
        new_m, new_a = [], []
        for pi in pairs:
            scores_into(c + 1, 1 - slot, only=pi)
            values_from(c - 1, 1 - slot, alphas, only=pi)
            m_new, alpha = softmax_pair(slot, pi, ms[pi])
            new_m.append(m_new)
            new_a.append(alpha)
        return tuple(new_m), tuple(new_a)

    def past_body(i, carry):
        ms, alphas = carry
        ms, alphas = half_trip(2 * i, 0, ms, alphas)
        return half_trip(2 * i + 1, 1, ms, alphas)

    scores_into(0, 0)
    for pi in pairs:
        pr_ref[1, pi] = jnp.zeros((kc, pw), BF16)
    n_trips = (n_past + 1) // 2
    _, alphas = lax.fori_loop(0, n_trips, past_body,
                              (tuple(m0), tuple(jnp.ones((1, pw), F32) for _ in pairs)))
    values_from(2 * n_trips - 1, 1, alphas)

    head_out = [None] * N_HEADS
    for pi in pairs:
        h0 = 2 * pi
        mixed = (gate_row(0, h0) * o_cmps[pi] + gate_row(1, h0) * normalised(acc_ref[pi], pi)
                 + gate_row(2, h0) * o_wins[pi])
        head_out[h0] = mixed[:, :tq]
        head_out[h0 + 1] = mixed[:, tq:]

    o_ref[0] = jnp.concatenate(head_out, axis=0).T.astype(BF16)


def _attention(q, kvsw, kcmp, vcmp, gates, batch, seq):
    n_rows = seq // CMP_STRIDE
    n_pairs = N_HEADS // 2
    ovt = jnp.asarray(_overlap_matrix(n_rows, seq // SLC_BLOCK))
    tile = lambda b, t: (b, t, 0)
    per_b = lambda b, t: (b, 0, 0)
    return pl.pallas_call(
        functools.partial(_attn_kernel, seq=seq),
        grid=(batch, seq // ATT_Q),
        in_specs=[pl.BlockSpec((1, ATT_Q, NSA_W), tile),
                  pl.BlockSpec((1, seq, 4 * KV_W), per_b),
                  pl.BlockSpec((1, n_rows, KV_W), per_b),
                  pl.BlockSpec((1, n_rows, KV_W), per_b),
                  pl.BlockSpec((ATT_Q, LANES), lambda b, t: (b * (seq // ATT_Q) + t, 0)),
                  pl.BlockSpec((N_KV, LANES, n_rows), lambda b, t: (0, 0, 0))],
        out_specs=pl.BlockSpec((1, ATT_Q, NSA_W), tile),
        out_shape=jax.ShapeDtypeStruct((batch, seq, NSA_W), BF16),
        scratch_shapes=[pltpu.VMEM((N_KV, seq, LANES), BF16),
                        pltpu.VMEM((N_KV, WINDOW + seq, LANES), BF16),
                        pltpu.VMEM((N_KV, LANES, seq), BF16),
                        pltpu.VMEM((N_KV, LANES, WINDOW + seq), BF16),
                        pltpu.VMEM((N_KV, LANES, n_rows), BF16),
                        pltpu.VMEM((n_pairs, LANES, 2 * ATT_Q), BF16),
                        pltpu.VMEM((n_pairs, LANES, 2 * ATT_Q), F32),
                        pltpu.VMEM((2, n_pairs, ATT_PAST, 2 * ATT_Q), F32),
                        pltpu.VMEM((2, n_pairs, ATT_PAST, 2 * ATT_Q), BF16)],
        compiler_params=_cparams(("arbitrary", "arbitrary")),
        name="nsa_attention",
    )(q.reshape(batch, seq, NSA_W), kvsw.reshape(batch, seq, 4 * KV_W), kcmp, vcmp, gates, ovt)


def _even_out_kernel(a_ref, sa_ref, usb_ref, vn_ref, ws_ref, bias_ref, w_ref, g_ref, x_ref, o_ref):
    lane = lax.broadcasted_iota(jnp.int32, (SGU_CHUNK, LANES), 1)
    low = lane < SGU_GROUP_DIM
    ri = lax.broadcasted_iota(jnp.int32, (SGU_CHUNK, SGU_CHUNK), 0)
    ci = lax.broadcasted_iota(jnp.int32, (SGU_CHUNK, SGU_CHUNK), 1)
    tril = ci <= ri
    wmix = [jnp.where(tril, ws_ref[g], 0.0).astype(BF16) for g in range(SGU_GROUPS)]
    zero = jnp.zeros((), BF16)
    left = (a_ref[...].astype(F32) * sa_ref[...].astype(F32)).astype(BF16)
    rights = []
    for c in range(OUT_ROWS // SGU_CHUNK):
        rows = slice(c * SGU_CHUNK, (c + 1) * SGU_CHUNK)
        blocks = []
        for p in range(SGU_W // LANES):
            cols = slice(p * LANES, (p + 1) * LANES)
            vb = vn_ref[rows, cols]
            mixed = (_dot(wmix[2 * p], jnp.where(low, vb, zero))
                     + _dot(wmix[2 * p + 1], jnp.where(low, zero, vb))
                     + bias_ref[:, cols])
            blocks.append((usb_ref[rows, cols].astype(F32) * mixed).astype(BF16))
        rights.append(jnp.concatenate(blocks, axis=1))
    right = jnp.concatenate(rights, axis=0)
    y = _dot(left, w_ref[0:NSA_W, :]) + _dot(right, w_ref[NSA_W:, :])
    o_ref[...] = x_ref[...] + _rms(y, g_ref[...])


def _even_out_streamed(a_hbm, sa_hbm, usb_hbm, vn_hbm, ws_ref, bias_ref, w_ref, g_ref, x_hbm, o_hbm):
    t, d = x_hbm.shape
    row = lambda i: (i, 0)
    deep = pl.Buffered(EVEN_OUT_BUFFERS)

    def body(a_ref, sa_ref, usb_ref, vn_ref, x_ref, o_ref):
        _even_out_kernel(a_ref, sa_ref, usb_ref, vn_ref, ws_ref, bias_ref, w_ref, g_ref, x_ref, o_ref)

    pltpu.emit_pipeline(
        body,
        grid=(t // OUT_ROWS,),
        in_specs=[pl.BlockSpec((OUT_ROWS, NSA_W), row, pipeline_mode=deep),
                  pl.BlockSpec((OUT_ROWS, NSA_W), row, pipeline_mode=deep),
                  pl.BlockSpec((OUT_ROWS, SGU_W), row, pipeline_mode=deep),
                  pl.BlockSpec((OUT_ROWS, SGU_W), row, pipeline_mode=deep),
                  pl.BlockSpec((OUT_ROWS, d), row, pipeline_mode=deep)],
        out_specs=[pl.BlockSpec((OUT_ROWS, d), row)],
    )(a_hbm, sa_hbm, usb_hbm, vn_hbm, x_hbm, o_hbm)


def _even_out(a, sa, usb, vn, sgu_w, sgu_b, w_out, g_post, x2):
    t, d = x2.shape
    bias = jnp.repeat(sgu_b.T, SGU_GROUP_DIM, axis=1)
    hbm = pl.BlockSpec(memory_space=pl.ANY)
    vmem = pl.BlockSpec(memory_space=pltpu.VMEM)
    return pl.pallas_call(
        _even_out_streamed,
        in_specs=[hbm, hbm, hbm, hbm, vmem, vmem, vmem, vmem, hbm],
        out_specs=hbm,
        out_shape=jax.ShapeDtypeStruct((t, d), F32),
        compiler_params=pltpu.CompilerParams(vmem_limit_bytes=VMEM_LIMIT_BYTES),
        name="even_out",
    )(a, sa, usb, vn, sgu_w, bias, w_out.astype(BF16), g_post.reshape(1, d), x2)


def _odd_kernel(x_ref, gpre_ref, win_ref, dww_ref, dwb_ref, lng_ref, lnb_ref, wout_ref, gpost_ref,
                o_ref, ypad_ref, conv_ref, sz_ref):
    d = x_ref.shape[-1]
    x = x_ref[0]
    h = _rms(x, gpre_ref[...]).astype(BF16)
    n_slices = d // ODD_COLS

    @pl.when(pl.program_id(1) == 0)
    def _():
        ypad_ref[:, 0:CONV_HALO, :] = jnp.zeros((d // LANES, CONV_HALO, LANES), F32)

    def project_glu(cs):
        c0, c1 = cs * ODD_COLS, (cs + 1) * ODD_COLS
        a = _dot(h, win_ref[:, c0:c1])
        gl = _dot(h, win_ref[:, d + c0:d + c1])
        y = a * jax.nn.sigmoid(gl)
        for j in range(ODD_COLS // LANES):
            ypad_ref[c0 // LANES + j, CONV_HALO:, :] = y[:, j * LANES:(j + 1) * LANES]

    def project_gate(cs):
        c0, c1 = cs * ODD_COLS, (cs + 1) * ODD_COLS
        sz_ref[:, c0:c1] = jax.nn.silu(_dot(h, win_ref[:, 2 * d + c0:2 * d + c1])).astype(BF16)

    shift = CONV_HALO - (CONV_K - 1)

    def conv(cs):
        for lt in range(cs * ODD_COLS // LANES, (cs + 1) * ODD_COLS // LANES):
            cols = slice(lt * LANES, (lt + 1) * LANES)
            taps = [jnp.broadcast_to(dww_ref[k:k + 1, cols], (SUBLANES, LANES)) for k in range(CONV_K)]
            bias = jnp.broadcast_to(dwb_ref[:, cols], (SUBLANES, LANES))
            blocks = CONV_RB // SUBLANES
            for base in range(0, ODD_ROWS, CONV_RB):
                parts = [[None] * CONV_CHAINS for _ in range(blocks)]
                for s in range(CONV_K + CONV_RB - SUBLANES):
                    win = ypad_ref[lt, base + shift + s:base + shift + s + SUBLANES, :]
                    for blk in range(blocks):
                        k = s - blk * SUBLANES
                        if 0 <= k < CONV_K:
                            term = taps[k] * win
                            old = parts[blk][k % CONV_CHAINS]
                            parts[blk][k % CONV_CHAINS] = term if old is None else old + term
                for blk in range(blocks):
                    r0 = base + blk * SUBLANES
                    conv_ref[r0:r0 + SUBLANES, cols] = functools.reduce(jnp.add, parts[blk]) + bias

    project_glu(0)
    for cs in range(n_slices):
        if cs + 1 < n_slices:
            project_glu(cs + 1)
        else:
            for gs in range(n_slices):
                project_gate(gs)
        conv(cs)
    ypad_ref[:, 0:CONV_HALO, :] = ypad_ref[:, ODD_ROWS:ODD_ROWS + CONV_HALO, :]

    y = jax.nn.silu(_layer_norm(conv_ref[...], lng_ref[...], lnb_ref[...]))
    out = _dot((y * sz_ref[...].astype(F32)).astype(BF16), wout_ref[...])
    o_ref[0] = x + _rms(out, gpost_ref[...])


def _odd_layer(x3, g_pre, w_in, dw_w, dw_b, ln_g, ln_b, w_out, g_post):
    batch, seq, d = x3.shape
    tile = lambda b, t: (b, t, 0)
    fixed = lambda b, t: (0, 0)
    vec = lambda v: v.reshape(1, d)
    return pl.pallas_call(
        _odd_kernel,
        grid=(batch, seq // ODD_ROWS),
        in_specs=[pl.BlockSpec((1, ODD_ROWS, d), tile), pl.BlockSpec((1, d), fixed),
                  pl.BlockSpec((d, 3 * d), fixed),
                  pl.BlockSpec((CONV_K, d), fixed),
                  pl.BlockSpec((1, d), fixed), pl.BlockSpec((1, d), fixed), pl.BlockSpec((1, d), fixed),
                  pl.BlockSpec((d, d), fixed), pl.BlockSpec((1, d), fixed)],
        out_specs=pl.BlockSpec((1, ODD_ROWS, d), tile),
        out_shape=jax.ShapeDtypeStruct((batch, seq, d), F32),
        scratch_shapes=[pltpu.VMEM((d // LANES, CONV_HALO + ODD_ROWS, LANES), F32),
                        pltpu.VMEM((ODD_ROWS, d), F32), pltpu.VMEM((ODD_ROWS, d), BF16)],
        compiler_params=_cparams(("arbitrary", "arbitrary")),
        name="odd_layer",
    )(x3, vec(g_pre), w_in.astype(BF16), dw_w, vec(dw_b), vec(ln_g), vec(ln_b),
      w_out.astype(BF16), vec(g_post))


def kernel(x, norm_pre, norm_post, even_w_in, even_cmp_k_pe, even_cmp_k_w1, even_cmp_k_w2,
           even_cmp_v_pe, even_cmp_v_w1, even_cmp_v_w2, even_sgu_ln_g, even_sgu_ln_b,
           even_sgu_w, even_sgu_b, even_w_out, odd_w_in, odd_dw_w, odd_dw_b, odd_ln_g,
           odd_ln_b, odd_w_out):
    batch, seq, d = x.shape
    depth = norm_pre.shape[0]
    for i in range(depth):
        li = i // 2
        if i % 2 == 0:
            x2 = x.reshape(batch * seq, d)
            q, kc, vc, kvsw, sa, usb, vn, gates = _even_in(
                x2, norm_pre[i], even_w_in[li], even_sgu_ln_g[li], even_sgu_ln_b[li])
            cw = _compress_weights(even_cmp_k_pe[li], even_cmp_k_w1[li], even_cmp_k_w2[li],
                                   even_cmp_v_pe[li], even_cmp_v_w1[li], even_cmp_v_w2[li])
            kcmp, vcmp = _compress(kc, vc, batch, seq, cw)
            a = _attention(q, kvsw, kcmp, vcmp, gates, batch, seq)
            x = _even_out(a.reshape(batch * seq, NSA_W), sa, usb, vn, even_sgu_w[li], even_sgu_b[li],
                          even_w_out[li], norm_post[i], x2).reshape(batch, seq, d)
        else:
            x = _odd_layer(x, norm_pre[i], odd_w_in[li], odd_dw_w[li], odd_dw_b[li],
                           odd_ln_g[li], odd_ln_b[li], odd_w_out[li], norm_post[i])
    return x
```
